```python
import jax, jax.numpy as jnp
from jax import lax
import numpy as np

D_MODEL = 1024
BATCH = 4
SEQ = 4096
DEPTH = 4
DEC_BATCH = 32
DEC_SEQ = 8
PAST_LEN = 8192
PAGE_SIZE = 128

N_A = DEPTH // 2
N_B = DEPTH - N_A
D_RNN = D_MODEL
N_RNN_BLOCKS = 8
RNN_BLOCK = D_RNN // N_RNN_BLOCKS
CONV_W = 4
RGLRU_C = 8.0
N_HEADS = 16
HEAD_DIM = D_MODEL // N_HEADS
D_ATTN = N_HEADS * HEAD_DIM
Q_BLOCK = 128
EPS = 1e-6

kernel_name = "hawk_fox_yoco_step"


def rms_norm(x, g):
    xf = x.astype(jnp.float32)
    y = xf * lax.rsqrt(jnp.mean(xf * xf, axis=-1, keepdims=True) + EPS)
    return (y * g.astype(jnp.float32)).astype(x.dtype)


def causal_conv(u, buf, w, b):
    T = u.shape[1]
    up = jnp.concatenate([buf.astype(u.dtype), u], axis=1)
    y = up[:, CONV_W - 1:CONV_W - 1 + T] * w[CONV_W - 1] + b
    for j in range(CONV_W - 1):
        y = y + up[:, j:j + T] * w[j]
    return y, up[:, -(CONV_W - 1):]


def rg_lru(u, h0, w_ga, b_ga, w_gx, b_gx, lam):
    B, T, _ = u.shape
    uf = u.astype(jnp.float32)
    ub = uf.reshape(B, T, N_RNN_BLOCKS, RNN_BLOCK)
    r = jax.nn.sigmoid(jnp.einsum('btnc,ncd->btnd', ub, w_ga.astype(jnp.float32)).reshape(B, T, D_RNN)
                       + b_ga.astype(jnp.float32))
    i = jax.nn.sigmoid(jnp.einsum('btnc,ncd->btnd', ub, w_gx.astype(jnp.float32)).reshape(B, T, D_RNN)
                       + b_gx.astype(jnp.float32))
    log_a = -RGLRU_C * r * jax.nn.softplus(-lam.astype(jnp.float32))
    a = jnp.exp(log_a)
    inp = jnp.sqrt(-jnp.expm1(2.0 * log_a)) * (i * uf)

    def combine(left, right):
        a1, b1 = left
        a2, b2 = right
        return a1 * a2, a2 * b1 + b2

    a_cum, b_cum = lax.associative_scan(combine, (a, inp), axis=1)
    h = a_cum * h0.astype(jnp.float32)[:, None, :] + b_cum
    return h.astype(u.dtype), h[:, -1].astype(u.dtype)


def recurrent_mixer(x, h0, conv_buf, w_in, conv_w, conv_b, w_ga, b_ga, w_gx, b_gx, lam, w_out):
    z = x @ w_in
    u, g = z[..., :D_RNN], z[..., D_RNN:]
    u, new_buf = causal_conv(u, conv_buf, conv_w, conv_b)
    h, h_last = rg_lru(u, h0, w_ga, b_ga, w_gx, b_gx, lam)
    return (h * jax.nn.silu(g)) @ w_out, h_last, new_buf


def shared_kv(x, kv_norm, w_kv, b_f):
    B, T, _ = x.shape
    z = rms_norm(x, kv_norm) @ w_kv
    k = z[..., :D_ATTN].reshape(B, T, N_HEADS, HEAD_DIM)
    v = z[..., D_ATTN:2 * D_ATTN].reshape(B, T, N_HEADS, HEAD_DIM)
    logf = jax.nn.log_sigmoid(z[..., 2 * D_ATTN:].astype(jnp.float32) + b_f.astype(jnp.float32))
    return k, v, logf


def fox_block(q, k, v, cq, ck, q_pos, k_pos):
    s = jnp.einsum('bqhd,bkhd->bhqk', q, k).astype(jnp.float32) * (HEAD_DIM ** -0.5)
    s = s + jnp.swapaxes(cq, 1, 2)[:, :, :, None] - jnp.swapaxes(ck, 1, 2)[:, :, None, :]
    mask = k_pos[None, :] <= q_pos[:, None]
    s = jnp.where(mask[None, None], s, -jnp.inf)
    p = jax.nn.softmax(s, axis=-1)
    return jnp.einsum('bhqk,bkhd->bqhd', p.astype(v.dtype), v)


def fox_attention(q, k, v, cq, ck, q_pos, k_pos):
    B, Tq = q.shape[:2]
    blk = min(Q_BLOCK, Tq)
    nb = Tq // blk
    qb = jnp.swapaxes(q.reshape(B, nb, blk, N_HEADS, HEAD_DIM), 0, 1)
    cqb = jnp.swapaxes(cq.reshape(B, nb, blk, N_HEADS), 0, 1)
    pb = q_pos.reshape(nb, blk)
    out = lax.map(lambda a: fox_block(a[0], k, v, a[1], ck, a[2], k_pos), (qb, cqb, pb))
    return jnp.swapaxes(out, 0, 1).reshape(B, Tq, N_HEADS, HEAD_DIM)


def fox_mixer(x, k, v, cq, ck, q_pos, k_pos, w_in, w_out):
    B, T, _ = x.shape
    z = x @ w_in
    q = z[..., :D_ATTN].reshape(B, T, N_HEADS, HEAD_DIM)
    g = z[..., D_ATTN:]
    o = fox_attention(q, k, v, cq, ck, q_pos, k_pos).reshape(B, T, D_ATTN)
    return (o * jax.nn.silu(g)) @ w_out


def run_group(x, rnn_state, conv_state, past_k, past_v, past_logf, pos0,
              a_pre_norm, a_post_norm, a_w_in, a_conv_w, a_conv_b, a_w_ga, a_b_ga,
              a_w_gx, a_b_gx, a_lambda, a_w_out, kv_norm, w_kv, b_f,
              b_pre_norm, b_post_norm, b_w_in, b_w_out):
    B, T, _ = x.shape
    new_rnn, new_conv = [], []
    kv = None
    for layer in range(DEPTH):
        if layer < N_A:
            l = layer
            y, h_last, buf = recurrent_mixer(rms_norm(x, a_pre_norm[l]), rnn_state[l], conv_state[l],
                                             a_w_in[l], a_conv_w[l], a_conv_b[l], a_w_ga[l], a_b_ga[l],
                                             a_w_gx[l], a_b_gx[l], a_lambda[l], a_w_out[l])
            x = x + rms_norm(y, a_post_norm[l])
            new_rnn.append(h_last)
            new_conv.append(buf)
            if layer == N_A - 1:
                k_new, v_new, logf_new = shared_kv(x, kv_norm, w_kv, b_f)
                if past_k is None:
                    k_all, v_all, logf_all = k_new, v_new, logf_new
                else:
                    k_all = jnp.concatenate([past_k.astype(k_new.dtype), k_new], axis=1)
                    v_all = jnp.concatenate([past_v.astype(v_new.dtype), v_new], axis=1)
                    logf_all = jnp.concatenate([past_logf.astype(jnp.float32), logf_new], axis=1)
                P = k_all.shape[1] - T
                c_all = jnp.cumsum(logf_all, axis=1)
                kv = (k_all, v_all, c_all[:, P:], c_all,
                      pos0 + jnp.arange(T), jnp.arange(P + T))
        else:
            l = layer - N_A
            k_all, v_all, cq, ck, q_pos, k_pos = kv
            y = fox_mixer(rms_norm(x, b_pre_norm[l]), k_all, v_all, cq, ck, q_pos, k_pos,
                          b_w_in[l], b_w_out[l])
            x = x + rms_norm(y, b_post_norm[l])
    return x, k_new, v_new, logf_new, jnp.stack(new_rnn), jnp.stack(new_conv)


def setup_inputs(seed: int = 0) -> dict:
    key = jax.random.key(seed)
    ks = list(jax.random.split(key, 32))
    f32 = jnp.float32
    n_pages = PAST_LEN // PAGE_SIZE
    n_used = DEC_BATCH * n_pages
    n_pool = n_used + n_used // 4
    nrm = lambda k, s, sc: jax.random.normal(k, s, f32) * sc
    x_prompt = nrm(ks[0], (BATCH, SEQ, D_MODEL), 1.0)
    x_sample = nrm(ks[1], (DEC_BATCH, DEC_SEQ, D_MODEL), 1.0)
    b_f = jax.random.uniform(ks[2], (N_HEADS,), f32, 1.0, 6.0)
    cache_k = nrm(ks[3], (n_pool, PAGE_SIZE, N_HEADS, HEAD_DIM), 1.0)
    cache_v = nrm(ks[4], (n_pool, PAGE_SIZE, N_HEADS, HEAD_DIM), 1.0)
    cache_logf = jax.nn.log_sigmoid(nrm(ks[5], (n_pool, PAGE_SIZE, N_HEADS), 1.0) + b_f)
    state_rnn = nrm(ks[6], (N_A, DEC_BATCH, D_RNN), 0.5)
    state_conv = nrm(ks[7], (N_A, DEC_BATCH, CONV_W - 1, D_RNN), 1.0)
    page_table = jax.random.permutation(ks[8], n_pool)[:n_used].reshape(DEC_BATCH, n_pages).astype(jnp.int32)
    p_lam = jax.random.uniform(ks[9], (N_A, D_RNN), f32, 0.9, 0.999)
    return {
        "x_prompt": x_prompt,
        "x_sample": x_sample,
        "cache_k": cache_k,
        "cache_v": cache_v,
        "cache_logf": cache_logf,
        "state_rnn": state_rnn,
        "state_conv": state_conv,
        "page_table": page_table,
        "a_pre_norm": 1.0 + nrm(ks[10], (N_A, D_MODEL), 0.05),
        "a_post_norm": 1.0 + nrm(ks[11], (N_A, D_MODEL), 0.05),
        "a_w_in": nrm(ks[12], (N_A, D_MODEL, 2 * D_RNN), D_MODEL ** -0.5),
        "a_conv_w": nrm(ks[13], (N_A, CONV_W, D_RNN), CONV_W ** -0.5),
        "a_conv_b": nrm(ks[14], (N_A, D_RNN), 0.02),
        "a_w_ga": nrm(ks[15], (N_A, N_RNN_BLOCKS, RNN_BLOCK, RNN_BLOCK), RNN_BLOCK ** -0.5),
        "a_b_ga": nrm(ks[16], (N_A, D_RNN), 0.1),
        "a_w_gx": nrm(ks[17], (N_A, N_RNN_BLOCKS, RNN_BLOCK, RNN_BLOCK), RNN_BLOCK ** -0.5),
        "a_b_gx": nrm(ks[18], (N_A, D_RNN), 0.1),
        "a_lambda": jnp.log(p_lam) - jnp.log1p(-p_lam),
        "a_w_out": nrm(ks[19], (N_A, D_RNN, D_MODEL), D_RNN ** -0.5),
        "kv_norm": 1.0 + nrm(ks[20], (D_MODEL,), 0.05),
        "w_kv": nrm(ks[21], (D_MODEL, 2 * D_ATTN + N_HEADS), D_MODEL ** -0.5),
        "b_f": b_f,
        "b_pre_norm": 1.0 + nrm(ks[22], (N_B, D_MODEL), 0.05),
        "b_post_norm": 1.0 + nrm(ks[23], (N_B, D_MODEL), 0.05),
        "b_w_in": nrm(ks[24], (N_B, D_MODEL, 2 * D_ATTN), D_MODEL ** -0.5),
        "b_w_out": nrm(ks[25], (N_B, D_ATTN, D_MODEL), D_ATTN ** -0.5),
    }


def reference(x_prompt, x_sample, cache_k, cache_v, cache_logf, state_rnn, state_conv, page_table,
              a_pre_norm, a_post_norm, a_w_in, a_conv_w, a_conv_b, a_w_ga, a_b_ga, a_w_gx, a_b_gx,
              a_lambda, a_w_out, kv_norm, w_kv, b_f, b_pre_norm, b_post_norm, b_w_in, b_w_out):
    weights = (a_pre_norm, a_post_norm, a_w_in, a_conv_w, a_conv_b, a_w_ga, a_b_ga, a_w_gx, a_b_gx,
               a_lambda, a_w_out, kv_norm, w_kv, b_f, b_pre_norm, b_post_norm, b_w_in, b_w_out)
    Bp = x_prompt.shape[0]
    rnn0 = jnp.zeros((N_A, Bp, D_RNN), x_prompt.dtype)
    conv0 = jnp.zeros((N_A, Bp, CONV_W - 1, D_RNN), x_prompt.dtype)
    y_prompt, k_p, v_p, logf_p, rnn_p, conv_p = run_group(
        x_prompt, rnn0, conv0, None, None, None, 0, *weights)
    Bs, n_pages = page_table.shape
    past_len = n_pages * cache_k.shape[1]
    past_k = cache_k[page_table].reshape(Bs, past_len, N_HEADS, HEAD_DIM)
    past_v = cache_v[page_table].reshape(Bs, past_len, N_HEADS, HEAD_DIM)
    past_logf = cache_logf[page_table].reshape(Bs, past_len, N_HEADS)
    y_sample, k_s, v_s, logf_s, rnn_s, conv_s = run_group(
        x_sample, state_rnn, state_conv, past_k, past_v, past_logf, past_len, *weights)
    return (y_prompt, y_sample, k_p, v_p, logf_p, rnn_p, conv_p, k_s, v_s, logf_s, rnn_s, conv_s)
```

```python
import functools

import jax
import jax.numpy as jnp
from jax import lax
from jax.experimental import pallas as pl
from jax.experimental.pallas import tpu as pltpu

F32 = jnp.float32
BF16 = jnp.bfloat16

EPS = 1e-6
RGLRU_C = 8.0
CONV_W = 4
LANES = 128
SUBLANES = 8
N_AUG = 3
VMEM_LIMIT = 56 * 1024 * 1024

_NT = (((1,), (1,)), ((), ()))


def _rms(x, g):
    ms = jnp.mean(x * x, axis=-1, keepdims=True)
    return x * lax.rsqrt(ms + EPS) * g


def _softplus(x):
    return jnp.maximum(x, 0.0) + jnp.log1p(jnp.exp(-jnp.abs(x)))


def _log_sigmoid(x):
    return jnp.minimum(x, 0.0) - jnp.log1p(jnp.exp(-jnp.abs(x)))


def _expm1_of_square(x, u):
    d = u - 1.0
    edge = (d == 0.0) | (d == -1.0)
    r = d * x / jnp.log(jnp.where(edge, 0.5, u))
    return jnp.where(d == 0.0, x, jnp.where(d == -1.0, -1.0, r))


def _silu(x):
    return x * jax.nn.sigmoid(x)


def _split3(x):
    hi = x.astype(BF16).astype(F32)
    r1 = x - hi
    mid = r1.astype(BF16).astype(F32)
    lo = (r1 - mid).astype(BF16).astype(F32)
    return hi, mid, lo


def _exact_dot(ones_lhs, x):
    hi, mid, lo = _split3(x)
    d = lambda p: jnp.dot(ones_lhs, p.astype(BF16), preferred_element_type=F32)
    return d(hi) + d(mid) + d(lo)


def _exact_dot_rhs(x, ones_rhs):
    hi, mid, lo = _split3(x)
    d = lambda p: jnp.dot(p.astype(BF16), ones_rhs, preferred_element_type=F32)
    return d(hi) + d(mid) + d(lo)


def _const_spec(shape):
    nd = len(shape)
    return pl.BlockSpec(shape, lambda *_: (0,) * nd)


def _layer_a_kernel(x_ref, h0_ref, c0_ref, pre_ref, post_ref, win_ref, cw_ref, cb_ref,
                    wg_ref, bga_ref, bgx_ref, lam_ref, wout_ref,
                    y_ref, hl_ref, cn_ref,
                    ubuf, a_s, b_s, hcar, *, tt, nb, off):
    t = pl.program_id(1)
    rows = tt * nb
    d = x_ref.shape[-1]
    tail = (CONV_W - 1) * nb
    nblk = wg_ref.shape[0]
    blk = d // nblk

    @pl.when(t == 0)
    def _():
        hcar[...] = h0_ref[0]
        ubuf[off - tail:off, :] = c0_ref[0]

    x = x_ref[0]
    xn = _rms(x, pre_ref[...])
    z = jnp.dot(xn.astype(BF16), win_ref[...], preferred_element_type=F32)
    u = z[:, :d]
    g = z[:, d:]

    ubuf[off:off + rows, :] = u
    cw = cw_ref[...]
    uc = u * cw[CONV_W - 1:CONV_W, :] + cb_ref[...]
    for j in range(CONV_W - 1):
        lo = off - (CONV_W - 1 - j) * nb
        uc = uc + ubuf[lo:lo + rows, :] * cw[j:j + 1, :]
    ubuf[off - tail:off, :] = ubuf[off + rows - tail:off + rows, :]

    sp = _softplus(-lam_ref[...])
    ucb = uc.astype(BF16)
    for n in range(nblk):
        sl = slice(n * blk, (n + 1) * blk)
        gz = jnp.dot(ucb[:, sl], wg_ref[n], preferred_element_type=F32)
        r = jax.nn.sigmoid(gz[:, :blk] + bga_ref[:, sl])
        i = jax.nn.sigmoid(gz[:, blk:] + bgx_ref[:, sl])
        log_a = (-RGLRU_C) * r * sp[:, sl]
        a = jnp.exp(log_a)
        a_s[:, sl] = a
        b_s[:, sl] = jnp.sqrt(-_expm1_of_square(2.0 * log_a, a * a)) * (i * uc[:, sl])

    if nb == 1:
        def body(i, h):
            base = pl.multiple_of(i * SUBLANES, SUBLANES)
            for j in range(SUBLANES):
                a = a_s[pl.ds(base + j, 1), :]
                b = b_s[pl.ds(base + j, 1), :]
                h = a * h + b
                b_s[pl.ds(base + j, 1), :] = h
            return h
        h = lax.fori_loop(0, tt // SUBLANES, body, hcar[...])
    else:
        h = hcar[...]
        for s in range(tt):
            sl = slice(s * nb, (s + 1) * nb)
            h = a_s[sl, :] * h + b_s[sl, :]
            b_s[sl, :] = h
    hcar[...] = h
    hl_ref[0] = h
    cn_ref[0] = ubuf[off - tail:off, :]

    hg = b_s[...] * _silu(g)
    y = jnp.dot(hg.astype(BF16), wout_ref[...], preferred_element_type=F32)
    y_ref[0] = x + _rms(y, post_ref[...])


def _layer_a(x, h0, c0, pre, post, win, cw, cb, wg, bga, bgx, lam, wout, *, tt, nb):
    g_, rows_total, d = x.shape
    rows = tt * nb
    nt = rows_total // rows
    assert nt * rows == rows_total and tt >= CONV_W - 1
    assert (nb == 1 and tt % SUBLANES == 0) or nb % SUBLANES == 0
    tail = (CONV_W - 1) * nb
    off = -(-tail // SUBLANES) * SUBLANES
    row = lambda v: v.reshape(1, d)
    kern = functools.partial(_layer_a_kernel, tt=tt, nb=nb, off=off)
    return pl.pallas_call(
        kern,
        grid=(g_, nt),
        in_specs=[
            pl.BlockSpec((1, rows, d), lambda g, t: (g, t, 0)),
            pl.BlockSpec((1, nb, d), lambda g, t: (g, 0, 0)),
            pl.BlockSpec((1, tail, d), lambda g, t: (g, 0, 0)),
            _const_spec((1, d)), _const_spec((1, d)),
            _const_spec(win.shape), _const_spec(cw.shape), _const_spec((1, d)),
            _const_spec(wg.shape), _const_spec((1, d)), _const_spec((1, d)), _const_spec((1, d)),
            _const_spec(wout.shape),
        ],
        out_specs=[
            pl.BlockSpec((1, rows, d), lambda g, t: (g, t, 0)),
            pl.BlockSpec((1, nb, d), lambda g, t: (g, 0, 0)),
            pl.BlockSpec((1, tail, d), lambda g, t: (g, 0, 0)),
        ],
        out_shape=[
            jax.ShapeDtypeStruct((g_, rows_total, d), F32),
            jax.ShapeDtypeStruct((g_, nb, d), F32),
            jax.ShapeDtypeStruct((g_, tail, d), F32),
        ],
        scratch_shapes=[
            pltpu.VMEM((off + rows, d), F32),
            pltpu.VMEM((rows, d), F32),
            pltpu.VMEM((rows, d), F32),
            pltpu.VMEM((nb, d), F32),
        ],
        compiler_params=pltpu.CompilerParams(
            dimension_semantics=("parallel", "arbitrary"), vmem_limit_bytes=VMEM_LIMIT),
        name="rglru_layer",
    )(x, h0, c0, row(pre), row(post), win, cw, row(cb), wg, row(bga), row(bgx), row(lam), wout)


def _head_block(vals, col, h, hd, negate):
    rows = vals.shape[0]
    pair = (h * hd) // LANES
    odd = ((h * hd) % LANES) != 0
    blk = vals[:, pair * LANES:(pair + 1) * LANES]
    lane = lax.broadcasted_iota(jnp.int32, (rows, LANES), 1)
    a0 = 0 if odd else hd
    data = (lane >= hd) if odd else (lane < hd)
    c = -col if negate else col
    hi, mid, lo = _split3(c)
    ones_lo = a0 + (0 if negate else N_AUG)
    val_lo = a0 + (N_AUG if negate else 0)
    aug = jnp.where(lane == val_lo, hi,
                    jnp.where(lane == val_lo + 1, mid,
                              jnp.where(lane == val_lo + 2, lo,
                                        jnp.where((lane >= ones_lo) & (lane < ones_lo + N_AUG), 1.0, 0.0))))
    return jnp.where(data, blk, aug).astype(BF16)


def _kv_kernel(x_ref, g_ref, wk_ref, wv_ref, wf_ref, bf_ref, *refs, aug, n_heads):
    if aug:
        k_ref, v_ref, lf_ref, c_ref, kp_ref, vb_ref, carry = refs
    else:
        k_ref, v_ref, lf_ref = refs
    x = x_ref[0]
    tt, d = x.shape
    hd = d // n_heads
    xn = _rms(x, g_ref[...]).astype(BF16)
    k = jnp.dot(xn, wk_ref[...], preferred_element_type=F32)
    v = jnp.dot(xn, wv_ref[...], preferred_element_type=F32)
    zf = jnp.dot(xn, wf_ref[...], preferred_element_type=F32)
    lf = _log_sigmoid(zf + bf_ref[...])
    k_ref[0] = k
    v_ref[0] = v
    lf_ref[0] = lf[:, :n_heads]
    if aug:
        @pl.when(pl.program_id(1) == 0)
        def _():
            carry[...] = jnp.zeros_like(carry)
        r = lax.broadcasted_iota(jnp.int32, (tt, tt), 0)
        c = lax.broadcasted_iota(jnp.int32, (tt, tt), 1)
        tri = jnp.where(r >= c, 1.0, 0.0).astype(BF16)
        cs = _exact_dot(tri, lf) + carry[...]
        carry[...] = cs[tt - 1:tt, :]
        c_ref[0] = cs[:, :n_heads]
        for h in range(n_heads):
            kp_ref[0, h] = _head_block(k, cs[:, h:h + 1], h, hd, negate=True)
        for p in range(d // LANES):
            vb_ref[0, p] = v[:, p * LANES:(p + 1) * LANES].astype(BF16)


def _kv_proj(x, g, wk, wv, wf, bfp, *, tt, aug, n_heads):
    b, t, d = x.shape
    nt = t // tt
    assert nt * tt == t
    np_ = d // LANES
    in_specs = [
        pl.BlockSpec((1, tt, d), lambda i, j: (i, j, 0)),
        _const_spec((1, d)), _const_spec(wk.shape), _const_spec(wv.shape),
        _const_spec(wf.shape), _const_spec((1, LANES)),
    ]
    out_specs = [
        pl.BlockSpec((1, tt, d), lambda i, j: (i, j, 0)),
        pl.BlockSpec((1, tt, d), lambda i, j: (i, j, 0)),
        pl.BlockSpec((1, tt, n_heads), lambda i, j: (i, j, 0)),
    ]
    out_shape = [
        jax.ShapeDtypeStruct((b, t, d), F32),
        jax.ShapeDtypeStruct((b, t, d), F32),
        jax.ShapeDtypeStruct((b, t, n_heads), F32),
    ]
    scratch = []
    if aug:
        out_specs += [
            pl.BlockSpec((1, tt, n_heads), lambda i, j: (i, j, 0)),
            pl.BlockSpec((1, n_heads, tt, LANES), lambda i, j: (i, 0, j, 0)),
            pl.BlockSpec((1, np_, tt, LANES), lambda i, j: (i, 0, j, 0)),
        ]
        out_shape += [
            jax.ShapeDtypeStruct((b, t, n_heads), F32),
            jax.ShapeDtypeStruct((b, n_heads, t, LANES), BF16),
            jax.ShapeDtypeStruct((b, np_, t, LANES), BF16),
        ]
        scratch = [pltpu.VMEM((1, LANES), F32)]
    return pl.pallas_call(
        functools.partial(_kv_kernel, aug=aug, n_heads=n_heads),
        grid=(b, nt),
        in_specs=in_specs, out_specs=out_specs, out_shape=out_shape,
        scratch_shapes=scratch,
        compiler_params=pltpu.CompilerParams(
            dimension_semantics=("parallel", "arbitrary"), vmem_limit_bytes=VMEM_LIMIT),
        name="kv_proj",
    )(x, g.reshape(1, d), wk, wv, wf, bfp)


def _qg_kernel(x_ref, g_ref, w_ref, *refs, aug, n_heads):
    if aug:
        c_ref, q_ref, gate_ref = refs
    else:
        q_ref, gate_ref = refs
    x = x_ref[0]
    d = x.shape[-1]
    hd = d // n_heads
    xn = _rms(x, g_ref[...]).astype(BF16)
    z = jnp.dot(xn, w_ref[...], preferred_element_type=F32)
    q = z[:, :d] * (hd ** -0.5)
    gate_ref[0] = z[:, d:]
    if aug:
        cs = c_ref[0]
        for h in range(n_heads):
            q_ref[0, h] = _head_block(q, cs[:, h:h + 1], h, hd, negate=False)
    else:
        q_ref[0] = q


def _qg_proj(x, g, w, c, *, tt, aug, n_heads):
    b, t, d = x.shape
    nt = t // tt
    assert nt * tt == t
    in_specs = [pl.BlockSpec((1, tt, d), lambda i, j: (i, j, 0)),
                _const_spec((1, d)), _const_spec(w.shape)]
    args = [x, g.reshape(1, d), w]
    if aug:
        in_specs.append(pl.BlockSpec((1, tt, n_heads), lambda i, j: (i, j, 0)))
        args.append(c)
        q_spec = pl.BlockSpec((1, n_heads, tt, LANES), lambda i, j: (i, 0, j, 0))
        q_shape = jax.ShapeDtypeStruct((b, n_heads, t, LANES), BF16)
    else:
        q_spec = pl.BlockSpec((1, tt, d), lambda i, j: (i, j, 0))
        q_shape = jax.ShapeDtypeStruct((b, t, d), F32)
    return pl.pallas_call(
        functools.partial(_qg_kernel, aug=aug, n_heads=n_heads),
        grid=(b, nt),
        in_specs=in_specs,
        out_specs=[q_spec, pl.BlockSpec((1, tt, d), lambda i, j: (i, j, 0))],
        out_shape=[q_shape, jax.ShapeDtypeStruct((b, t, d), F32)],
        compiler_params=pltpu.CompilerParams(
            dimension_semantics=("parallel", "parallel"), vmem_limit_bytes=VMEM_LIMIT),
        name="qg_proj",
    )(*args)


def _attn_kernel(qi_tab, ki_tab, q_ref, k_ref, v_ref, o_ref, m_s, l_s, acc_s, *, hd):
    step = pl.program_id(1)
    qi = qi_tab[step]
    ki = ki_tab[step]
    n_heads, tq, _ = q_ref.shape[1:]
    tk = k_ref.shape[2]
    n_pairs = v_ref.shape[1]

    @pl.when(ki == 0)
    def _():
        m_s[...] = jnp.full(m_s.shape, -jnp.inf, F32)
        l_s[...] = jnp.zeros_like(l_s)
        acc_s[...] = jnp.zeros_like(acc_s)

    first = lax.broadcasted_iota(jnp.int32, (tq, LANES), 1) < hd

    def run(diag):
        def pair_body(p, carry):
            alphas, pvs = [], []
            for hh in range(2):
                h = 2 * p + hh
                s = lax.dot_general(q_ref[0, h], k_ref[0, h], _NT, preferred_element_type=F32)
                if diag:
                    keep = (lax.broadcasted_iota(jnp.int32, (tq, tk), 1)
                            <= lax.broadcasted_iota(jnp.int32, (tq, tk), 0))
                    s = jnp.where(keep, s, -jnp.inf)
                m_prev = m_s[h]
                m_new = jnp.maximum(m_prev, jnp.max(s, axis=1, keepdims=True))
                alpha = jnp.exp(m_prev - m_new)
                pe = jnp.exp(s - m_new)
                l_s[h] = alpha * l_s[h] + jnp.sum(pe, axis=1, keepdims=True)
                m_s[h] = m_new
                pvs.append(jnp.dot(pe.astype(BF16), v_ref[0, p], preferred_element_type=F32))
                alphas.append(alpha)
            acc_s[p] = (acc_s[p] * jnp.where(first, alphas[0], alphas[1])
                        + jnp.where(first, pvs[0], pvs[1]))
            return carry

        lax.fori_loop(0, n_pairs, pair_body, 0)

    @pl.when(ki < qi)
    def _():
        run(False)

    @pl.when(ki == qi)
    def _():
        run(True)
        for p in range(n_pairs):
            inv = jnp.where(first, 1.0 / l_s[2 * p], 1.0 / l_s[2 * p + 1])
            o_ref[0, p] = acc_s[p] * inv


def _prompt_attention(qp, kp, vb, *, blk, hd):
    b, n_heads, t, _ = qp.shape
    n_pairs = vb.shape[1]
    nq = t // blk
    assert nq * blk == t
    qi_l, ki_l = [], []
    for i in range(nq):
        for j in range(i + 1):
            qi_l.append(i)
            ki_l.append(j)
    qi_tab = jnp.asarray(qi_l, jnp.int32)
    ki_tab = jnp.asarray(ki_l, jnp.int32)
    grid_spec = pltpu.PrefetchScalarGridSpec(
        num_scalar_prefetch=2,
        grid=(b, len(qi_l)),
        in_specs=[
            pl.BlockSpec((1, n_heads, blk, LANES), lambda i, s, qt, kt: (i, 0, qt[s], 0)),
            pl.BlockSpec((1, n_heads, blk, LANES), lambda i, s, qt, kt: (i, 0, kt[s], 0)),
            pl.BlockSpec((1, n_pairs, blk, LANES), lambda i, s, qt, kt: (i, 0, kt[s], 0)),
        ],
        out_specs=pl.BlockSpec((1, n_pairs, blk, LANES), lambda i, s, qt, kt: (i, 0, qt[s], 0)),
        scratch_shapes=[
            pltpu.VMEM((n_heads, blk, 1), F32),
            pltpu.VMEM((n_heads, blk, 1), F32),
            pltpu.VMEM((n_pairs, blk, LANES), F32),
        ],
    )
    return pl.pallas_call(
        functools.partial(_attn_kernel, hd=hd),
        grid_spec=grid_spec,
        out_shape=jax.ShapeDtypeStruct((b, n_pairs, t, LANES), F32),
        compiler_params=pltpu.CompilerParams(
            dimension_semantics=("parallel", "arbitrary"), vmem_limit_bytes=VMEM_LIMIT),
        name="prompt_attention",
    )(qi_tab, ki_tab, qp, kp, vb)


def _out_kernel(o_ref, gate_ref, x_ref, w_ref, post_ref, y_ref, *, pairs):
    if pairs:
        o = jnp.concatenate([o_ref[0, p] for p in range(o_ref.shape[1])], axis=1)
    else:
        o = o_ref[0]
    hg = o * _silu(gate_ref[0])
    y = jnp.dot(hg.astype(BF16), w_ref[...], preferred_element_type=F32)
    y_ref[0] = x_ref[0] + _rms(y, post_ref[...])


def _out_proj(o, gate, x, w, post, *, tt, pairs):
    b, t, d = x.shape
    nt = t // tt
    assert nt * tt == t
    row_spec = pl.BlockSpec((1, tt, d), lambda i, j: (i, j, 0))
    if pairs:
        o_spec = pl.BlockSpec((1, o.shape[1], tt, LANES), lambda i, j: (i, 0, j, 0))
    else:
        o_spec = row_spec
    return pl.pallas_call(
        functools.partial(_out_kernel, pairs=pairs),
        grid=(b, nt),
        in_specs=[o_spec, row_spec, row_spec, _const_spec(w.shape), _const_spec((1, d))],
        out_specs=row_spec,
        out_shape=jax.ShapeDtypeStruct((b, t, d), F32),
        compiler_params=pltpu.CompilerParams(
            dimension_semantics=("parallel", "parallel"), vmem_limit_bytes=VMEM_LIMIT),
        name="attn_out_proj",
    )(o, gate, x, w, post.reshape(1, d))


N_SLOTS = 6


def _decode_kernel(pt_ref, q_ref, kn_ref, vn_ref, lfn_ref, kc_hbm, vc_hbm, lfc_hbm, o_ref,
                   kvbuf, lbuf, sem_kv, sem_l, s_ref, p_ref, acc_ref, *, n_heads):
    seq = pl.program_id(0)
    n_pages = pt_ref.shape[1]
    page, d = kvbuf.shape[1:]
    nq = q_ref.shape[1]
    hd = d // n_heads
    rows = n_heads * nq
    assert rows == LANES and page == LANES

    def kv_copy(src_hbm, page_id, slot):
        return pltpu.make_async_copy(src_hbm.at[page_id], kvbuf.at[slot], sem_kv.at[slot])

    def lf_copy(page_id, slot):
        return pltpu.make_async_copy(lfc_hbm.at[page_id], lbuf.at[slot], sem_l.at[slot])

    def start(i):
        slot = lax.rem(i, N_SLOTS)

        @pl.when(i < n_pages)
        def _():
            pid = pt_ref[seq, i]
            kv_copy(kc_hbm, pid, slot).start()
            lf_copy(pid, slot).start()

        @pl.when((i >= n_pages) & (i < 2 * n_pages))
        def _():
            kv_copy(vc_hbm, pt_ref[seq, i - n_pages], slot).start()

    for i in range(N_SLOTS):
        start(jnp.int32(i))

    q = q_ref[0]
    lane_head = lax.broadcasted_iota(jnp.int32, (nq, d), 1) // hd
    qbd = jnp.concatenate([jnp.where(lane_head == h, q, 0.0) for h in range(n_heads)],
                          axis=0).astype(BF16)

    r = lax.broadcasted_iota(jnp.int32, (page, 2 * page), 0)
    c = lax.broadcasted_iota(jnp.int32, (page, 2 * page), 1)
    tri_ones = jnp.where((r <= c) | (c >= page), 1.0, 0.0).astype(BF16)

    def head_rows(ct):
        return jnp.concatenate(
            [jnp.broadcast_to(ct[h:h + 1, :], (nq, ct.shape[1])) for h in range(n_heads)], axis=0)

    def k_body(i, carry):
        slot = lax.rem(i, N_SLOTS)
        kv_copy(kc_hbm, 0, slot).wait()
        lf_copy(0, slot).wait()
        kb = kvbuf[slot].astype(BF16)
        sc = lax.dot_general(qbd, kb, _NT, preferred_element_type=F32)
        cum = _exact_dot_rhs(lbuf[slot], tri_ones)
        ct = carry + cum[:, :page]
        s_ref[i] = sc - head_rows(ct)
        start(i + N_SLOTS)
        return carry + cum[:, page:]

    past_total = lax.fori_loop(0, n_pages, k_body, jnp.zeros((n_heads, page), F32))

    ct_new = past_total + _exact_dot_rhs(lfn_ref[0], tri_ones)[:, :page]
    kn = jnp.concatenate([kn_ref[0], jnp.zeros((page - nq, d), F32)], axis=0).astype(BF16)
    sc_new = lax.dot_general(qbd, kn, _NT, preferred_element_type=F32)
    key_idx = lax.broadcasted_iota(jnp.int32, (rows, page), 1)
    q_idx = lax.rem(lax.broadcasted_iota(jnp.int32, (rows, page), 0), nq)
    ct_rows = head_rows(ct_new)
    cq = jnp.sum(jnp.where(key_idx == q_idx, ct_rows, 0.0), axis=1, keepdims=True)
    s_ref[n_pages] = jnp.where(key_idx <= q_idx, sc_new - ct_rows, -jnp.inf)

    def max_body(i, m):
        return jnp.maximum(m, s_ref[i])
    m_el = lax.fori_loop(0, n_pages + 1, max_body, jnp.full((rows, page), -jnp.inf, F32))
    m = jnp.max(m_el, axis=1, keepdims=True) + cq

    def p_body(i, l):
        pe = jnp.exp((s_ref[i] + cq) - m)
        p_ref[i] = pe.astype(BF16)
        return l + pe
    l_el = lax.fori_loop(0, n_pages + 1, p_body, jnp.zeros((rows, page), F32))
    l = jnp.sum(l_el, axis=1, keepdims=True)

    vn = jnp.concatenate([vn_ref[0], jnp.zeros((page - nq, d), F32)], axis=0).astype(BF16)
    acc_ref[...] = jnp.dot(p_ref[n_pages], vn, preferred_element_type=F32)

    def v_body(i, carry):
        item = n_pages + i
        slot = lax.rem(item, N_SLOTS)
        kv_copy(vc_hbm, 0, slot).wait()
        vb = kvbuf[slot].astype(BF16)
        acc_ref[...] += jnp.dot(p_ref[i], vb, preferred_element_type=F32)
        start(item + N_SLOTS)
        return carry

    lax.fori_loop(0, n_pages, v_body, 0)

    acc = acc_ref[...] * (1.0 / l)
    out = jnp.zeros((nq, d), F32)
    for h in range(n_heads):
        out = out + jnp.where(lane_head == h, acc[h * nq:(h + 1) * nq, :], 0.0)
    o_ref[0] = out


def _decode_attention(page_table, q, k_new, v_new, lft_new, cache_k, cache_v, cache_lft, *, n_heads):
    b, nq, d = q.shape
    n_pages = page_table.shape[1]
    page = cache_k.shape[1]
    rows = n_heads * nq
    seq_spec = pl.BlockSpec((1, nq, d), lambda i, pt: (i, 0, 0))
    grid_spec = pltpu.PrefetchScalarGridSpec(
        num_scalar_prefetch=1,
        grid=(b,),
        in_specs=[
            seq_spec, seq_spec, seq_spec,
            pl.BlockSpec((1, n_heads, page), lambda i, pt: (i, 0, 0)),
            pl.BlockSpec(memory_space=pl.ANY),
            pl.BlockSpec(memory_space=pl.ANY),
            pl.BlockSpec(memory_space=pl.ANY),
        ],
        out_specs=seq_spec,
        scratch_shapes=[
            pltpu.VMEM((N_SLOTS, page, d), F32),
            pltpu.VMEM((N_SLOTS, n_heads, page), F32),
            pltpu.SemaphoreType.DMA((N_SLOTS,)),
            pltpu.SemaphoreType.DMA((N_SLOTS,)),
            pltpu.VMEM((n_pages + 1, rows, page), F32),
            pltpu.VMEM((n_pages + 1, rows, page), BF16),
            pltpu.VMEM((rows, d), F32),
        ],
    )
    return pl.pallas_call(
        functools.partial(_decode_kernel, n_heads=n_heads),
        grid_spec=grid_spec,
        out_shape=jax.ShapeDtypeStruct((b, nq, d), F32),
        compiler_params=pltpu.CompilerParams(
            dimension_semantics=("arbitrary",), vmem_limit_bytes=VMEM_LIMIT),
        name="decode_attention",
    )(page_table, q, k_new, v_new, lft_new, cache_k, cache_v, cache_lft)


def kernel(x_prompt, x_sample, cache_k, cache_v, cache_logf, state_rnn, state_conv, page_table,
           a_pre_norm, a_post_norm, a_w_in, a_conv_w, a_conv_b, a_w_ga, a_b_ga, a_w_gx, a_b_gx,
           a_lambda, a_w_out, kv_norm, w_kv, b_f, b_pre_norm, b_post_norm, b_w_in, b_w_out):
    n_a = a_w_in.shape[0]
    n_b = b_w_in.shape[0]
    d = x_prompt.shape[-1]
    n_heads = b_f.shape[0]
    hd = d // n_heads
    assert 2 * hd == LANES and n_heads <= LANES

    a_win = a_w_in.astype(BF16)
    a_wout = a_w_out.astype(BF16)
    a_wg = jnp.concatenate([a_w_ga, a_w_gx], axis=-1).astype(BF16)
    wk = w_kv[:, :d].astype(BF16)
    wv = w_kv[:, d:2 * d].astype(BF16)
    wf = jnp.pad(w_kv[:, 2 * d:], ((0, 0), (0, LANES - n_heads))).astype(BF16)
    bfp = jnp.pad(b_f, (0, LANES - n_heads)).reshape(1, LANES)
    b_win = b_w_in.astype(BF16)
    b_wout = b_w_out.astype(BF16)

    def a_layer(l, x, h0, c0, tt, nb):
        return _layer_a(x, h0, c0, a_pre_norm[l], a_post_norm[l], a_win[l], a_conv_w[l], a_conv_b[l],
                        a_wg[l], a_b_ga[l], a_b_gx[l], a_lambda[l], a_wout[l], tt=tt, nb=nb)

    bp, tp, _ = x_prompt.shape
    x = x_prompt
    rnn_p, conv_p = [], []
    for l in range(n_a):
        x, hl, cn = a_layer(l, x, jnp.zeros((bp, 1, d), F32), jnp.zeros((bp, CONV_W - 1, d), F32),
                            tt=256, nb=1)
        rnn_p.append(hl.reshape(bp, d))
        conv_p.append(cn)
    k_p, v_p, lf_p, c_p, kaug_p, vb_p = _kv_proj(x, kv_norm, wk, wv, wf, bfp, tt=512, aug=True,
                                                 n_heads=n_heads)
    for l in range(n_b):
        qaug, gate = _qg_proj(x, b_pre_norm[l], b_win[l], c_p, tt=512, aug=True, n_heads=n_heads)
        o = _prompt_attention(qaug, kaug_p, vb_p, blk=512, hd=hd)
        x = _out_proj(o, gate, x, b_wout[l], b_post_norm[l], tt=512, pairs=True)
    y_prompt = x

    bs, ts, _ = x_sample.shape
    rows_s = bs * ts
    n_pool, page = cache_k.shape[:2]
    x = jnp.swapaxes(x_sample, 0, 1).reshape(1, rows_s, d)
    rnn_s, conv_s = [], []
    for l in range(n_a):
        c0 = jnp.swapaxes(state_conv[l], 0, 1).reshape(1, (CONV_W - 1) * bs, d)
        x, hl, cn = a_layer(l, x, state_rnn[l].reshape(1, bs, d), c0, tt=ts, nb=bs)
        rnn_s.append(hl.reshape(bs, d))
        conv_s.append(jnp.swapaxes(cn.reshape(CONV_W - 1, bs, d), 0, 1))
    x = jnp.swapaxes(x.reshape(ts, bs, d), 0, 1).reshape(1, rows_s, d)
    k_s, v_s, lf_s = _kv_proj(x, kv_norm, wk, wv, wf, bfp, tt=rows_s, aug=False, n_heads=n_heads)
    lft_new = jnp.pad(jnp.swapaxes(lf_s.reshape(bs, ts, n_heads), 1, 2),
                      ((0, 0), (0, 0), (0, page - ts)))
    cache_k3 = cache_k.reshape(n_pool, page, d)
    cache_v3 = cache_v.reshape(n_pool, page, d)
    cache_lft = jnp.swapaxes(cache_logf, 1, 2)
    for l in range(n_b):
        q, gate = _qg_proj(x, b_pre_norm[l], b_win[l], None, tt=rows_s, aug=False, n_heads=n_heads)
        o = _decode_attention(page_table, q.reshape(bs, ts, d), k_s.reshape(bs, ts, d),
                              v_s.reshape(bs, ts, d), lft_new, cache_k3, cache_v3, cache_lft,
                              n_heads=n_heads)
        x = _out_proj(o.reshape(1, rows_s, d), gate, x, b_wout[l], b_post_norm[l], tt=rows_s,
                      pairs=False)
    y_sample = x.reshape(bs, ts, d)

    return (y_prompt, y_sample,
            k_p.reshape(bp, tp, n_heads, hd), v_p.reshape(bp, tp, n_heads, hd), lf_p,
            jnp.stack(rnn_p), jnp.stack(conv_p),
            k_s.reshape(bs, ts, n_heads, hd), v_s.reshape(bs, ts, n_heads, hd),
            lf_s.reshape(bs, ts, n_heads),
            jnp.stack(rnn_s), jnp.stack(conv_s))
```

```python
import functools

import jax
import jax.numpy as jnp
from jax import lax
from jax.experimental import pallas as pl
from jax.experimental.pallas import tpu as pltpu

F32 = jnp.float32
BF16 = jnp.bfloat16

EPS = 1e-6
RGLRU_C = 8.0
CONV_W = 4
LANES = 128
SUBLANES = 8
N_AUG = 3
VMEM_LIMIT = 56 * 1024 * 1024

_NT = (((1,), (1,)), ((), ()))
_TN = (((0,), (0,)), ((), ()))


def _rms(x, g):
    ms = jnp.mean(x * x, axis=-1, keepdims=True)
    return x * lax.rsqrt(ms + EPS) * g


def _softplus(x):
    return jnp.maximum(x, 0.0) + jnp.log1p(jnp.exp(-jnp.abs(x)))


def _log_sigmoid(x):
    return jnp.minimum(x, 0.0) - jnp.log1p(jnp.exp(-jnp.abs(x)))


def _expm1_of_square(x, u):
    d = u - 1.0
    edge = (d == 0.0) | (d == -1.0)
    r = d * x / jnp.log(jnp.where(edge, 0.5, u))
    return jnp.where(d == 0.0, x, jnp.where(d == -1.0, -1.0, r))


def _silu(x):
    return x * jax.nn.sigmoid(x)


def _split3(x):
    hi = x.astype(BF16).astype(F32)
    r1 = x - hi
    mid = r1.astype(BF16).astype(F32)
    lo = (r1 - mid).astype(BF16).astype(F32)
    return hi, mid, lo


def _exact_dot(ones_lhs, x):
    hi, mid, lo = _split3(x)
    d = lambda p: jnp.dot(ones_lhs, p.astype(BF16), preferred_element_type=F32)
    return d(hi) + d(mid) + d(lo)


def _exact_dot_rhs(x, ones_rhs):
    hi, mid, lo = _split3(x)
    d = lambda p: jnp.dot(p.astype(BF16), ones_rhs, preferred_element_type=F32)
    return d(hi) + d(mid) + d(lo)


def _const_spec(shape):
    nd = len(shape)
    return pl.BlockSpec(shape, lambda *_: (0,) * nd)


def _params(*sem):
    return pltpu.CompilerParams(dimension_semantics=sem, vmem_limit_bytes=VMEM_LIMIT)


def _layer_a_kernel(x_ref, h0_ref, c0_ref, pre_ref, post_ref, win_ref, cw_ref, cb_ref,
                    wg_ref, bga_ref, bgx_ref, lam_ref, wout_ref,
                    y_ref, hl_ref, cn_ref,
                    ubuf, a_s, b_s, hcar, *, tt, nb, off):
    t = pl.program_id(1)
    rows = tt * nb
    d = x_ref.shape[-1]
    tail = (CONV_W - 1) * nb
    nblk = wg_ref.shape[0]
    blk = d // nblk

    @pl.when(t == 0)
    def _():
        hcar[...] = h0_ref[0]
        ubuf[off - tail:off, :] = c0_ref[0]

    x = x_ref[0]
    xn = _rms(x, pre_ref[...])
    z = jnp.dot(xn.astype(BF16), win_ref[...], preferred_element_type=F32)
    u = z[:, :d]
    g = z[:, d:]

    ubuf[off:off + rows, :] = u
    cw = cw_ref[...]
    uc = u * cw[CONV_W - 1:CONV_W, :] + cb_ref[...]
    for j in range(CONV_W - 1):
        lo = off - (CONV_W - 1 - j) * nb
        uc = uc + ubuf[lo:lo + rows, :] * cw[j:j + 1, :]
    ubuf[off - tail:off, :] = ubuf[off + rows - tail:off + rows, :]

    sp = _softplus(-lam_ref[...])
    ucb = uc.astype(BF16)
    for n in range(nblk):
        sl = slice(n * blk, (n + 1) * blk)
        gz = jnp.dot(ucb[:, sl], wg_ref[n], preferred_element_type=F32)
        r = jax.nn.sigmoid(gz[:, :blk] + bga_ref[:, sl])
        i = jax.nn.sigmoid(gz[:, blk:] + bgx_ref[:, sl])
        log_a = (-RGLRU_C) * r * sp[:, sl]
        a = jnp.exp(log_a)
        a_s[:, sl] = a
        b_s[:, sl] = jnp.sqrt(-_expm1_of_square(2.0 * log_a, a * a)) * (i * uc[:, sl])

    if nb == 1:
        def body(i, h):
            base = pl.multiple_of(i * SUBLANES, SUBLANES)
            for j in range(SUBLANES):
                a = a_s[pl.ds(base + j, 1), :]
                b = b_s[pl.ds(base + j, 1), :]
                h = a * h + b
                b_s[pl.ds(base + j, 1), :] = h
            return h
        h = lax.fori_loop(0, tt // SUBLANES, body, hcar[...])
    else:
        h = hcar[...]
        for s in range(tt):
            sl = slice(s * nb, (s + 1) * nb)
            h = a_s[sl, :] * h + b_s[sl, :]
            b_s[sl, :] = h
    hcar[...] = h
    hl_ref[0] = h
    cn_ref[0] = ubuf[off - tail:off, :]

    hg = b_s[...] * _silu(g)
    y = jnp.dot(hg.astype(BF16), wout_ref[...], preferred_element_type=F32)
    y_ref[0] = x + _rms(y, post_ref[...])


def _layer_a(x, h0, c0, pre, post, win, cw, cb, wg, bga, bgx, lam, wout, *, tt, nb):
    g_, rows_total, d = x.shape
    rows = tt * nb
    nt = rows_total // rows
    assert nt * rows == rows_total and tt >= CONV_W - 1
    assert (nb == 1 and tt % SUBLANES == 0) or nb % SUBLANES == 0
    tail = (CONV_W - 1) * nb
    off = -(-tail // SUBLANES) * SUBLANES
    row = lambda v: v.reshape(1, d)
    kern = functools.partial(_layer_a_kernel, tt=tt, nb=nb, off=off)
    return pl.pallas_call(
        kern,
        grid=(g_, nt),
        in_specs=[
            pl.BlockSpec((1, rows, d), lambda g, t: (g, t, 0)),
            pl.BlockSpec((1, nb, d), lambda g, t: (g, 0, 0)),
            pl.BlockSpec((1, tail, d), lambda g, t: (g, 0, 0)),
            _const_spec((1, d)), _const_spec((1, d)),
            _const_spec(win.shape), _const_spec(cw.shape), _const_spec((1, d)),
            _const_spec(wg.shape), _const_spec((1, d)), _const_spec((1, d)), _const_spec((1, d)),
            _const_spec(wout.shape),
        ],
        out_specs=[
            pl.BlockSpec((1, rows, d), lambda g, t: (g, t, 0)),
            pl.BlockSpec((1, nb, d), lambda g, t: (g, 0, 0)),
            pl.BlockSpec((1, tail, d), lambda g, t: (g, 0, 0)),
        ],
        out_shape=[
            jax.ShapeDtypeStruct((g_, rows_total, d), F32),
            jax.ShapeDtypeStruct((g_, nb, d), F32),
            jax.ShapeDtypeStruct((g_, tail, d), F32),
        ],
        scratch_shapes=[
            pltpu.VMEM((off + rows, d), F32),
            pltpu.VMEM((rows, d), F32),
            pltpu.VMEM((rows, d), F32),
            pltpu.VMEM((nb, d), F32),
        ],
        compiler_params=_params("parallel", "arbitrary"),
        name="rglru_layer",
    )(x, h0, c0, row(pre), row(post), win, cw, row(cb), wg, row(bga), row(bgx), row(lam), wout)


def _aug_base(h, hd):
    upper = ((h * hd) % LANES) != 0
    return (0 if upper else hd), upper


def _key_block(k, col, h, hd):
    rows = k.shape[0]
    pair = (h * hd) // LANES
    blk = k[:, pair * LANES:(pair + 1) * LANES]
    a0, upper = _aug_base(h, hd)
    lane = lax.broadcasted_iota(jnp.int32, (rows, LANES), 1)
    data = (lane >= hd) if upper else (lane < hd)
    hi, mid, lo = _split3(-col)
    v0 = a0 + N_AUG
    aug = jnp.where(lane == v0, hi,
                    jnp.where(lane == v0 + 1, mid,
                              jnp.where(lane == v0 + 2, lo,
                                        jnp.where((lane >= a0) & (lane < v0), 1.0, 0.0))))
    return jnp.where(data, blk, aug).astype(BF16)


def _query_block_t(qt, row, h, hd):
    cols = qt.shape[1]
    a0, upper = _aug_base(h, hd)
    sub = lax.broadcasted_iota(jnp.int32, (SUBLANES, cols), 0)
    hi, mid, lo = _split3(row)
    aug = jnp.where(sub == 0, hi,
                    jnp.where(sub == 1, mid,
                              jnp.where(sub == 2, lo,
                                        jnp.where(sub < 2 * N_AUG, 1.0, 0.0))))
    pad = jnp.zeros((LANES - hd - SUBLANES, cols), F32)
    data = qt[h * hd:(h + 1) * hd, :]
    parts = [aug, pad, data] if upper else [data, aug, pad]
    return jnp.concatenate(parts, axis=0).astype(BF16)


def _kv_prompt_kernel(x_ref, g_ref, wk_ref, wkt_ref, wvt_ref, wf_ref, wft_ref, bfr_ref, bfc_ref,
                      kt_ref, vt_ref, lft_ref, ct_ref, kp_ref, vtb_ref, carry_r, carry_c,
                      *, n_heads):
    x = x_ref[0]
    tt, d = x.shape
    hd = d // n_heads
    xn = _rms(x, g_ref[...]).astype(BF16)
    kt = lax.dot_general(wkt_ref[...], xn, _NT, preferred_element_type=F32)
    vt = lax.dot_general(wvt_ref[...], xn, _NT, preferred_element_type=F32)
    k = jnp.dot(xn, wk_ref[...], preferred_element_type=F32)
    lf = _log_sigmoid(jnp.dot(xn, wf_ref[...], preferred_element_type=F32) + bfr_ref[...])
    lft = _log_sigmoid(lax.dot_general(wft_ref[...], xn, _NT, preferred_element_type=F32)
                       + bfc_ref[...])
    kt_ref[0] = kt
    vt_ref[0] = vt
    vtb_ref[0] = vt.astype(BF16)
    lft_ref[0] = lft

    @pl.when(pl.program_id(1) == 0)
    def _():
        carry_r[...] = jnp.zeros_like(carry_r)
        carry_c[...] = jnp.zeros_like(carry_c)

    r = lax.broadcasted_iota(jnp.int32, (tt, tt), 0)
    c = lax.broadcasted_iota(jnp.int32, (tt, tt), 1)
    cs = _exact_dot(jnp.where(r >= c, 1.0, 0.0).astype(BF16), lf) + carry_r[...]
    cst = _exact_dot_rhs(lft, jnp.where(r <= c, 1.0, 0.0).astype(BF16)) + carry_c[...]
    carry_r[...] = cs[tt - 1:tt, :]
    carry_c[...] = cst[:, tt - 1:tt]
    ct_ref[0] = cst
    for h in range(n_heads):
        kp_ref[0, h] = _key_block(k, cs[:, h:h + 1], h, hd)


def _kv_prompt(x, g, wk, wkt, wvt, wf, wft, bfr, bfc, *, tt, n_heads):
    b, t, d = x.shape
    nt = t // tt
    assert nt * tt == t
    feat = lambda n, dt: (pl.BlockSpec((1, n, tt), lambda i, j: (i, 0, j)),
                          jax.ShapeDtypeStruct((b, n, t), dt))
    outs = [feat(d, F32), feat(d, F32), feat(n_heads, F32), feat(n_heads, F32),
            (pl.BlockSpec((1, n_heads, tt, LANES), lambda i, j: (i, 0, j, 0)),
             jax.ShapeDtypeStruct((b, n_heads, t, LANES), BF16)),
            feat(d, BF16)]
    return pl.pallas_call(
        functools.partial(_kv_prompt_kernel, n_heads=n_heads),
        grid=(b, nt),
        in_specs=[pl.BlockSpec((1, tt, d), lambda i, j: (i, j, 0)), _const_spec((1, d)),
                  _const_spec(wk.shape), _const_spec(wkt.shape), _const_spec(wvt.shape),
                  _const_spec(wf.shape), _const_spec(wft.shape),
                  _const_spec(bfr.shape), _const_spec(bfc.shape)],
        out_specs=[o[0] for o in outs],
        out_shape=[o[1] for o in outs],
        scratch_shapes=[pltpu.VMEM((1, LANES), F32), pltpu.VMEM((n_heads, 1), F32)],
        compiler_params=_params("parallel", "arbitrary"),
        name="kv_prompt",
    )(x, g.reshape(1, d), wk, wkt, wvt, wf, wft, bfr, bfc)


def _qg_prompt_kernel(x_ref, g_ref, wt_ref, ct_ref, qt_ref, gt_ref, *, n_heads):
    x = x_ref[0]
    d = x.shape[-1]
    hd = d // n_heads
    xn = _rms(x, g_ref[...]).astype(BF16)
    zt = lax.dot_general(wt_ref[...], xn, _NT, preferred_element_type=F32)
    gt_ref[0] = zt[d:, :]
    qt = zt[:d, :] * (hd ** -0.5)
    ct = ct_ref[0]
    for h in range(n_heads):
        qt_ref[0, h] = _query_block_t(qt, ct[h:h + 1, :], h, hd)


def _qg_prompt(x, g, wt, ct, *, tt, n_heads):
    b, t, d = x.shape
    nt = t // tt
    assert nt * tt == t
    return pl.pallas_call(
        functools.partial(_qg_prompt_kernel, n_heads=n_heads),
        grid=(b, nt),
        in_specs=[pl.BlockSpec((1, tt, d), lambda i, j: (i, j, 0)), _const_spec((1, d)),
                  _const_spec(wt.shape),
                  pl.BlockSpec((1, n_heads, tt), lambda i, j: (i, 0, j))],
        out_specs=[pl.BlockSpec((1, n_heads, LANES, tt), lambda i, j: (i, 0, 0, j)),
                   pl.BlockSpec((1, d, tt), lambda i, j: (i, 0, j))],
        out_shape=[jax.ShapeDtypeStruct((b, n_heads, LANES, t), BF16),
                   jax.ShapeDtypeStruct((b, d, t), F32)],
        compiler_params=_params("parallel", "parallel"),
        name="qg_prompt",
    )(x, g.reshape(1, d), wt, ct)


HEADS_PER_ITER = 2


def _attn_kernel(qi_tab, ki_tab, q_ref, k_ref, v_ref, o_ref, m_s, l_s, acc_s, *, hd):
    step = pl.program_id(1)
    qi = qi_tab[step]
    ki = ki_tab[step]
    n_heads, _, tq = q_ref.shape[1:]
    tk = k_ref.shape[2]

    @pl.when(ki == 0)
    def _():
        m_s[...] = jnp.full(m_s.shape, -jnp.inf, F32)
        l_s[...] = jnp.zeros_like(l_s)
        acc_s[...] = jnp.zeros_like(acc_s)

    def run(diag):
        def head_body(h, carry):
            st = jnp.dot(k_ref[0, h], q_ref[0, h], preferred_element_type=F32)
            if diag:
                keep = (lax.broadcasted_iota(jnp.int32, (tk, tq), 0)
                        <= lax.broadcasted_iota(jnp.int32, (tk, tq), 1))
                st = jnp.where(keep, st, -jnp.inf)
            m_prev = m_s[h]
            m_new = jnp.maximum(m_prev, jnp.max(st, axis=0, keepdims=True))
            alpha = jnp.exp(m_prev - m_new)
            pt = jnp.exp(st - m_new)
            l_s[h] = alpha * l_s[h] + jnp.sum(pt, axis=0, keepdims=True)
            m_s[h] = m_new
            rows = pl.ds(pl.multiple_of(h * hd, hd), hd)
            pv = jnp.dot(v_ref[0, rows, :], pt.astype(BF16), preferred_element_type=F32)
            acc_s[rows, :] = acc_s[rows, :] * alpha + pv
            return carry

        def group_body(g, carry):
            for j in range(HEADS_PER_ITER):
                head_body(g * HEADS_PER_ITER + j, carry)
            return carry

        lax.fori_loop(0, n_heads // HEADS_PER_ITER, group_body, 0)

    @pl.when(ki < qi)
    def _():
        run(False)

    @pl.when(ki == qi)
    def _():
        run(True)
        for h in range(n_heads):
            rows = slice(h * hd, (h + 1) * hd)
            o_ref[0, rows, :] = acc_s[rows, :] * (1.0 / l_s[h])


def _prompt_attention(qt, kp, vtb, *, blk, hd):
    b, n_heads, _, t = qt.shape
    d = vtb.shape[1]
    nq = t // blk
    assert nq * blk == t
    qi_l, ki_l = [], []
    for i in range(nq):
        for j in range(i + 1):
            qi_l.append(i)
            ki_l.append(j)
    grid_spec = pltpu.PrefetchScalarGridSpec(
        num_scalar_prefetch=2,
        grid=(b, len(qi_l)),
        in_specs=[
            pl.BlockSpec((1, n_heads, LANES, blk), lambda i, s, qt_, kt_: (i, 0, 0, qt_[s])),
            pl.BlockSpec((1, n_heads, blk, LANES), lambda i, s, qt_, kt_: (i, 0, kt_[s], 0)),
            pl.BlockSpec((1, d, blk), lambda i, s, qt_, kt_: (i, 0, kt_[s])),
        ],
        out_specs=pl.BlockSpec((1, d, blk), lambda i, s, qt_, kt_: (i, 0, qt_[s])),
        scratch_shapes=[
            pltpu.VMEM((n_heads, 1, blk), F32),
            pltpu.VMEM((n_heads, 1, blk), F32),
            pltpu.VMEM((d, blk), F32),
        ],
    )
    return pl.pallas_call(
        functools.partial(_attn_kernel, hd=hd),
        grid_spec=grid_spec,
        out_shape=jax.ShapeDtypeStruct((b, d, t), F32),
        compiler_params=_params("parallel", "arbitrary"),
        name="prompt_attention",
    )(jnp.asarray(qi_l, jnp.int32), jnp.asarray(ki_l, jnp.int32), qt, kp, vtb)


def _out_prompt_kernel(ot_ref, gt_ref, x_ref, w_ref, post_ref, y_ref):
    hgt = ot_ref[0] * _silu(gt_ref[0])
    y = lax.dot_general(hgt.astype(BF16), w_ref[...], _TN, preferred_element_type=F32)
    y_ref[0] = x_ref[0] + _rms(y, post_ref[...])


def _out_prompt(ot, gt, x, wt, post, *, tt):
    b, t, d = x.shape
    nt = t // tt
    assert nt * tt == t
    row_spec = pl.BlockSpec((1, tt, d), lambda i, j: (i, j, 0))
    feat_spec = pl.BlockSpec((1, d, tt), lambda i, j: (i, 0, j))
    return pl.pallas_call(
        _out_prompt_kernel,
        grid=(b, nt),
        in_specs=[feat_spec, feat_spec, row_spec, _const_spec(wt.shape), _const_spec((1, d))],
        out_specs=row_spec,
        out_shape=jax.ShapeDtypeStruct((b, t, d), F32),
        compiler_params=_params("parallel", "parallel"),
        name="out_prompt",
    )(ot, gt, x, wt, post.reshape(1, d))


def _kv_sample_kernel(x_ref, g_ref, wk_ref, wv_ref, wf_ref, bfr_ref, k_ref, v_ref, lf_ref,
                      *, n_heads):
    xn = _rms(x_ref[...], g_ref[...]).astype(BF16)
    k_ref[...] = jnp.dot(xn, wk_ref[...], preferred_element_type=F32)
    v_ref[...] = jnp.dot(xn, wv_ref[...], preferred_element_type=F32)
    lf = _log_sigmoid(jnp.dot(xn, wf_ref[...], preferred_element_type=F32) + bfr_ref[...])
    lf_ref[...] = lf[:, :n_heads]


def _kv_sample(x, g, wk, wv, wf, bfr, *, n_heads):
    rows, d = x.shape
    return pl.pallas_call(
        functools.partial(_kv_sample_kernel, n_heads=n_heads),
        out_shape=[jax.ShapeDtypeStruct((rows, d), F32), jax.ShapeDtypeStruct((rows, d), F32),
                   jax.ShapeDtypeStruct((rows, n_heads), F32)],
        compiler_params=pltpu.CompilerParams(vmem_limit_bytes=VMEM_LIMIT),
        name="kv_sample",
    )(x, g.reshape(1, d), wk, wv, wf, bfr)


def _qg_sample_kernel(x_ref, g_ref, w_ref, q_ref, gate_ref, *, n_heads):
    d = x_ref.shape[-1]
    xn = _rms(x_ref[...], g_ref[...]).astype(BF16)
    z = jnp.dot(xn, w_ref[...], preferred_element_type=F32)
    q_ref[...] = z[:, :d] * ((d // n_heads) ** -0.5)
    gate_ref[...] = z[:, d:]


def _qg_sample(x, g, w, *, n_heads):
    rows, d = x.shape
    return pl.pallas_call(
        functools.partial(_qg_sample_kernel, n_heads=n_heads),
        out_shape=[jax.ShapeDtypeStruct((rows, d), F32), jax.ShapeDtypeStruct((rows, d), F32)],
        compiler_params=pltpu.CompilerParams(vmem_limit_bytes=VMEM_LIMIT),
        name="qg_sample",
    )(x, g.reshape(1, d), w)


def _out_sample_kernel(o_ref, gate_ref, x_ref, w_ref, post_ref, y_ref):
    hg = o_ref[...] * _silu(gate_ref[...])
    y = jnp.dot(hg.astype(BF16), w_ref[...], preferred_element_type=F32)
    y_ref[...] = x_ref[...] + _rms(y, post_ref[...])


def _out_sample(o, gate, x, w, post):
    rows, d = x.shape
    return pl.pallas_call(
        _out_sample_kernel,
        out_shape=jax.ShapeDtypeStruct((rows, d), F32),
        compiler_params=pltpu.CompilerParams(vmem_limit_bytes=VMEM_LIMIT),
        name="out_sample",
    )(o, gate, x, w, post.reshape(1, d))


PAGE_GROUP = 4
N_SLOTS = 2 * PAGE_GROUP


def _decode_kernel(pt_ref, q_ref, kn_ref, vn_ref, lfn_ref, kc_hbm, vc_hbm, lfc_hbm, o_ref,
                   kvbuf, lbuf, sem_kv, sem_l, s_ref, p_ref, acc_ref, qbd_ref, *, n_heads):
    seq = pl.program_id(0)
    n_seq = pl.num_programs(0)
    n_pages = pt_ref.shape[1]
    d, page = kvbuf.shape[1:]
    nq = q_ref.shape[1]
    hd = d // n_heads
    rows = n_heads * nq
    assert rows == LANES and page == LANES and n_pages % PAGE_GROUP == 0
    k_groups = n_pages // PAGE_GROUP
    n_groups = 2 * k_groups

    def kv_copy(src_hbm, page_id, slot):
        return pltpu.make_async_copy(src_hbm.at[page_id], kvbuf.at[slot], sem_kv.at[slot])

    def lf_copy(page_id, slot):
        return pltpu.make_async_copy(lfc_hbm.at[page_id], lbuf.at[slot], sem_l.at[slot])

    def group_base(grp):
        return lax.rem(grp, 2) * PAGE_GROUP

    def start_group(sq, grp):
        base = group_base(grp)

        @pl.when(grp < k_groups)
        def _():
            for j in range(PAGE_GROUP):
                pid = pt_ref[sq, grp * PAGE_GROUP + j]
                kv_copy(kc_hbm, pid, base + j).start()
                lf_copy(pid, base + j).start()

        @pl.when(grp >= k_groups)
        def _():
            for j in range(PAGE_GROUP):
                pid = pt_ref[sq, (grp - k_groups) * PAGE_GROUP + j]
                kv_copy(vc_hbm, pid, base + j).start()

    def refill(grp):
        @pl.when(grp < n_groups)
        def _():
            start_group(seq, grp)

        @pl.when((grp >= n_groups) & (seq + 1 < n_seq))
        def _():
            start_group(seq + 1, grp - n_groups)

    @pl.when(seq == 0)
    def _():
        start_group(seq, jnp.int32(0))
        start_group(seq, jnp.int32(1))

    q = q_ref[0]
    lane_head = lax.broadcasted_iota(jnp.int32, (nq, d), 1) // hd
    qbd_ref[...] = jnp.concatenate([jnp.where(lane_head == h, q, 0.0) for h in range(n_heads)],
                                   axis=0).astype(BF16)

    r = lax.broadcasted_iota(jnp.int32, (page, 2 * page), 0)
    c = lax.broadcasted_iota(jnp.int32, (page, 2 * page), 1)
    tri_ones = jnp.where((r <= c) | (c >= page), 1.0, 0.0).astype(BF16)

    def head_rows(ct):
        return jnp.concatenate(
            [jnp.broadcast_to(ct[h:h + 1, :], (nq, ct.shape[1])) for h in range(n_heads)], axis=0)

    def k_group(g, carry):
        base = group_base(g)
        for j in range(PAGE_GROUP):
            kv_copy(kc_hbm, 0, base + j).wait()
            lf_copy(0, base + j).wait()
        for j in range(PAGE_GROUP):
            sc = jnp.dot(qbd_ref[...], kvbuf[base + j].astype(BF16), preferred_element_type=F32)
            cum = _exact_dot_rhs(lbuf[base + j], tri_ones)
            s_ref[g * PAGE_GROUP + j] = sc - head_rows(carry + cum[:, :page])
            carry = carry + cum[:, page:]
        refill(g + 2)
        return carry

    past_total = lax.fori_loop(0, k_groups, k_group, jnp.zeros((n_heads, page), F32))

    ct_new = past_total + _exact_dot_rhs(lfn_ref[0], tri_ones)[:, :page]
    kn = jnp.concatenate([kn_ref[0], jnp.zeros((page - nq, d), F32)], axis=0).astype(BF16)
    sc_new = lax.dot_general(qbd_ref[...], kn, _NT, preferred_element_type=F32)
    key_idx = lax.broadcasted_iota(jnp.int32, (rows, page), 1)
    q_idx = lax.rem(lax.broadcasted_iota(jnp.int32, (rows, page), 0), nq)
    ct_rows = head_rows(ct_new)
    cq = jnp.sum(jnp.where(key_idx == q_idx, ct_rows, 0.0), axis=1, keepdims=True)
    s_ref[n_pages] = jnp.where(key_idx <= q_idx, sc_new - ct_rows, -jnp.inf)

    def max_body(i, m):
        return jnp.maximum(m, s_ref[i])
    m_el = lax.fori_loop(0, n_pages + 1, max_body, jnp.full((rows, page), -jnp.inf, F32))
    m = jnp.max(m_el, axis=1, keepdims=True) + cq

    def p_body(i, l):
        pe = jnp.exp((s_ref[i] + cq) - m)
        p_ref[i] = pe.astype(BF16)
        return l + pe
    l_el = lax.fori_loop(0, n_pages + 1, p_body, jnp.zeros((rows, page), F32))
    l = jnp.sum(l_el, axis=1, keepdims=True)

    vn = jnp.concatenate([vn_ref[0], jnp.zeros((page - nq, d), F32)], axis=0).astype(BF16)
    acc_ref[...] = jnp.dot(p_ref[n_pages], vn, preferred_element_type=F32)

    def v_group(g, carry):
        grp = k_groups + g
        base = group_base(grp)
        for j in range(PAGE_GROUP):
            kv_copy(vc_hbm, 0, base + j).wait()
        upd = None
        for j in range(PAGE_GROUP):
            pv = lax.dot_general(p_ref[g * PAGE_GROUP + j], kvbuf[base + j].astype(BF16), _NT,
                                 preferred_element_type=F32)
            upd = pv if upd is None else upd + pv
        acc_ref[...] += upd
        refill(grp + 2)
        return carry

    lax.fori_loop(0, k_groups, v_group, 0)

    acc = acc_ref[...] * (1.0 / l)
    out = jnp.zeros((nq, d), F32)
    for h in range(n_heads):
        out = out + jnp.where(lane_head == h, acc[h * nq:(h + 1) * nq, :], 0.0)
    o_ref[0] = out


def _decode_attention(page_table, q, k_new, v_new, lft_new, cache_kt, cache_vt, cache_lft, *, n_heads):
    b, nq, d = q.shape
    n_pages = page_table.shape[1]
    page = cache_kt.shape[2]
    rows = n_heads * nq
    seq_spec = pl.BlockSpec((1, nq, d), lambda i, pt: (i, 0, 0))
    grid_spec = pltpu.PrefetchScalarGridSpec(
        num_scalar_prefetch=1,
        grid=(b,),
        in_specs=[
            seq_spec, seq_spec, seq_spec,
            pl.BlockSpec((1, n_heads, page), lambda i, pt: (i, 0, 0)),
            pl.BlockSpec(memory_space=pl.ANY),
            pl.BlockSpec(memory_space=pl.ANY),
            pl.BlockSpec(memory_space=pl.ANY),
        ],
        out_specs=seq_spec,
        scratch_shapes=[
            pltpu.VMEM((N_SLOTS, d, page), F32),
            pltpu.VMEM((N_SLOTS, n_heads, page), F32),
            pltpu.SemaphoreType.DMA((N_SLOTS,)),
            pltpu.SemaphoreType.DMA((N_SLOTS,)),
            pltpu.VMEM((n_pages + 1, rows, page), F32),
            pltpu.VMEM((n_pages + 1, rows, page), BF16),
            pltpu.VMEM((rows, d), F32),
            pltpu.VMEM((rows, d), BF16),
        ],
    )
    return pl.pallas_call(
        functools.partial(_decode_kernel, n_heads=n_heads),
        grid_spec=grid_spec,
        out_shape=jax.ShapeDtypeStruct((b, nq, d), F32),
        compiler_params=_params("arbitrary"),
        name="decode_attention",
    )(page_table, q, k_new, v_new, lft_new, cache_kt, cache_vt, cache_lft)


def kernel(x_prompt, x_sample, cache_k, cache_v, cache_logf, state_rnn, state_conv, page_table,
           a_pre_norm, a_post_norm, a_w_in, a_conv_w, a_conv_b, a_w_ga, a_b_ga, a_w_gx, a_b_gx,
           a_lambda, a_w_out, kv_norm, w_kv, b_f, b_pre_norm, b_post_norm, b_w_in, b_w_out):
    n_a = a_w_in.shape[0]
    n_b = b_w_in.shape[0]
    d = x_prompt.shape[-1]
    n_heads = b_f.shape[0]
    hd = d // n_heads
    assert 2 * hd == LANES and n_heads <= LANES

    a_win = a_w_in.astype(BF16)
    a_wout = a_w_out.astype(BF16)
    a_wg = jnp.concatenate([a_w_ga, a_w_gx], axis=-1).astype(BF16)
    w_kv_t = w_kv.T.astype(BF16)
    wkt, wvt, wft = w_kv_t[:d], w_kv_t[d:2 * d], w_kv_t[2 * d:]
    wk = w_kv[:, :d].astype(BF16)
    wv = w_kv[:, d:2 * d].astype(BF16)
    wf = jnp.pad(w_kv[:, 2 * d:], ((0, 0), (0, LANES - n_heads))).astype(BF16)
    bfr = jnp.pad(b_f, (0, LANES - n_heads)).reshape(1, LANES)
    bfc = b_f.reshape(n_heads, 1)
    b_win = b_w_in.astype(BF16)
    b_wout = b_w_out.astype(BF16)
    b_win_t = jnp.swapaxes(b_w_in, 1, 2).astype(BF16)

    def a_layer(l, x, h0, c0, tt, nb):
        return _layer_a(x, h0, c0, a_pre_norm[l], a_post_norm[l], a_win[l], a_conv_w[l], a_conv_b[l],
                        a_wg[l], a_b_ga[l], a_b_gx[l], a_lambda[l], a_wout[l], tt=tt, nb=nb)

    bp, tp, _ = x_prompt.shape
    x = x_prompt
    rnn_p, conv_p = [], []
    for l in range(n_a):
        x, hl, cn = a_layer(l, x, jnp.zeros((bp, 1, d), F32), jnp.zeros((bp, CONV_W - 1, d), F32),
                            tt=256, nb=1)
        rnn_p.append(hl.reshape(bp, d))
        conv_p.append(cn)
    kt_p, vt_p, lft_p, ct_p, kaug_p, vtb_p = _kv_prompt(x, kv_norm, wk, wkt, wvt, wf, wft, bfr, bfc,
                                                        tt=512, n_heads=n_heads)
    for l in range(n_b):
        qt, gt = _qg_prompt(x, b_pre_norm[l], b_win_t[l], ct_p, tt=512, n_heads=n_heads)
        ot = _prompt_attention(qt, kaug_p, vtb_p, blk=512, hd=hd)
        x = _out_prompt(ot, gt, x, b_wout[l], b_post_norm[l], tt=512)
    y_prompt = x
    k_p = jnp.transpose(kt_p.reshape(bp, n_heads, hd, tp), (0, 3, 1, 2))
    v_p = jnp.transpose(vt_p.reshape(bp, n_heads, hd, tp), (0, 3, 1, 2))
    lf_p = jnp.swapaxes(lft_p, 1, 2)

    bs, ts, _ = x_sample.shape
    rows_s = bs * ts
    n_pool, page = cache_k.shape[:2]
    x = jnp.swapaxes(x_sample, 0, 1).reshape(1, rows_s, d)
    rnn_s, conv_s = [], []
    for l in range(n_a):
        c0 = jnp.swapaxes(state_conv[l], 0, 1).reshape(1, (CONV_W - 1) * bs, d)
        x, hl, cn = a_layer(l, x, state_rnn[l].reshape(1, bs, d), c0, tt=ts, nb=bs)
        rnn_s.append(hl.reshape(bs, d))
        conv_s.append(jnp.swapaxes(cn.reshape(CONV_W - 1, bs, d), 0, 1))
    x = jnp.swapaxes(x.reshape(ts, bs, d), 0, 1).reshape(rows_s, d)
    k_s, v_s, lf_s = _kv_sample(x, kv_norm, wk, wv, wf, bfr, n_heads=n_heads)
    lft_new = jnp.pad(jnp.swapaxes(lf_s.reshape(bs, ts, n_heads), 1, 2),
                      ((0, 0), (0, 0), (0, page - ts)))
    cache_kt = jnp.transpose(cache_k, (0, 2, 3, 1)).reshape(n_pool, d, page)
    cache_vt = jnp.transpose(cache_v, (0, 2, 3, 1)).reshape(n_pool, d, page)
    cache_lft = jnp.swapaxes(cache_logf, 1, 2)
    for l in range(n_b):
        q, gate = _qg_sample(x, b_pre_norm[l], b_win[l], n_heads=n_heads)
        o = _decode_attention(page_table, q.reshape(bs, ts, d), k_s.reshape(bs, ts, d),
                              v_s.reshape(bs, ts, d), lft_new, cache_kt, cache_vt, cache_lft,
                              n_heads=n_heads)
        x = _out_sample(o.reshape(rows_s, d), gate, x, b_wout[l], b_post_norm[l])
    y_sample = x.reshape(bs, ts, d)

    return (y_prompt, y_sample, k_p, v_p, lf_p,
            jnp.stack(rnn_p), jnp.stack(conv_p),
            k_s.reshape(bs, ts, n_heads, hd), v_s.reshape(bs, ts, n_heads, hd),
            lf_s.reshape(bs, ts, n_heads),
            jnp.stack(rnn_s), jnp.stack(conv_s))
```

```python
import functools

import jax
import jax.numpy as jnp
from jax import lax
from jax.experimental import pallas as pl
from jax.experimental.pallas import tpu as pltpu

F32 = jnp.float32
BF16 = jnp.bfloat16

EPS = 1e-6
RGLRU_C = 8.0
CONV_W = 4
LANES = 128
SUBLANES = 8
BF16_ROWS = 16
N_AUG = 3
LOG2E = 1.4426950408889634
VMEM_LIMIT = 56 * 1024 * 1024

_NT = (((1,), (1,)), ((), ()))
_TN = (((0,), (0,)), ((), ()))


def _rms(x, g):
    ms = jnp.mean(x * x, axis=-1, keepdims=True)
    return x * lax.rsqrt(ms + EPS) * g


def _softplus(x):
    return jnp.maximum(x, 0.0) + jnp.log1p(jnp.exp(-jnp.abs(x)))


def _log_sigmoid(x):
    return jnp.minimum(x, 0.0) - jnp.log1p(jnp.exp(-jnp.abs(x)))


def _expm1_of_square(x, u):
    d = u - 1.0
    edge = (d == 0.0) | (d == -1.0)
    r = d * x / jnp.log(jnp.where(edge, 0.5, u))
    return jnp.where(d == 0.0, x, jnp.where(d == -1.0, -1.0, r))


def _silu(x):
    return x * jax.nn.sigmoid(x)


def _split3(x):
    hi = x.astype(BF16).astype(F32)
    r1 = x - hi
    mid = r1.astype(BF16).astype(F32)
    lo = (r1 - mid).astype(BF16).astype(F32)
    return hi, mid, lo


def _exact_dot(ones_lhs, x):
    hi, mid, lo = _split3(x)
    d = lambda p: jnp.dot(ones_lhs, p.astype(BF16), preferred_element_type=F32)
    return d(hi) + d(mid) + d(lo)


def _exact_dot_rhs(x, ones_rhs):
    hi, mid, lo = _split3(x)
    d = lambda p: jnp.dot(p.astype(BF16), ones_rhs, preferred_element_type=F32)
    return d(hi) + d(mid) + d(lo)


def _const_spec(shape):
    nd = len(shape)
    return pl.BlockSpec(shape, lambda *_: (0,) * nd)


def _params(*sem):
    return pltpu.CompilerParams(dimension_semantics=sem, vmem_limit_bytes=VMEM_LIMIT)


def _layer_a_kernel(x_ref, h0_ref, c0_ref, pre_ref, post_ref, win_ref, cw_ref, cb_ref,
                    wg_ref, bga_ref, bgx_ref, lam_ref, wout_ref,
                    y_ref, hl_ref, cn_ref,
                    ubuf, a_s, b_s, hcar, *, tt, nb, off):
    t = pl.program_id(1)
    rows = tt * nb
    d = x_ref.shape[-1]
    tail = (CONV_W - 1) * nb
    nblk = wg_ref.shape[0]
    blk = d // nblk

    @pl.when(t == 0)
    def _():
        hcar[...] = h0_ref[0]
        ubuf[off - tail:off, :] = c0_ref[0]

    x = x_ref[0]
    xn = _rms(x, pre_ref[...])
    z = jnp.dot(xn.astype(BF16), win_ref[...], preferred_element_type=F32)
    u = z[:, :d]
    g = z[:, d:]

    ubuf[off:off + rows, :] = u
    cw = cw_ref[...]
    uc = u * cw[CONV_W - 1:CONV_W, :] + cb_ref[...]
    for j in range(CONV_W - 1):
        lo = off - (CONV_W - 1 - j) * nb
        uc = uc + ubuf[lo:lo + rows, :] * cw[j:j + 1, :]
    ubuf[off - tail:off, :] = ubuf[off + rows - tail:off + rows, :]

    sp = _softplus(-lam_ref[...])
    ucb = uc.astype(BF16)
    for n in range(nblk):
        sl = slice(n * blk, (n + 1) * blk)
        gz = jnp.dot(ucb[:, sl], wg_ref[n], preferred_element_type=F32)
        r = jax.nn.sigmoid(gz[:, :blk] + bga_ref[:, sl])
        i = jax.nn.sigmoid(gz[:, blk:] + bgx_ref[:, sl])
        log_a = (-RGLRU_C) * r * sp[:, sl]
        a = jnp.exp(log_a)
        a_s[:, sl] = a
        b_s[:, sl] = jnp.sqrt(-_expm1_of_square(2.0 * log_a, a * a)) * (i * uc[:, sl])

    if nb == 1:
        def body(i, h):
            base = pl.multiple_of(i * SUBLANES, SUBLANES)
            for j in range(SUBLANES):
                a = a_s[pl.ds(base + j, 1), :]
                b = b_s[pl.ds(base + j, 1), :]
                h = a * h + b
                b_s[pl.ds(base + j, 1), :] = h
            return h
        h = lax.fori_loop(0, tt // SUBLANES, body, hcar[...])
    else:
        h = hcar[...]
        for s in range(tt):
            sl = slice(s * nb, (s + 1) * nb)
            h = a_s[sl, :] * h + b_s[sl, :]
            b_s[sl, :] = h
    hcar[...] = h
    hl_ref[0] = h
    cn_ref[0] = ubuf[off - tail:off, :]

    hg = b_s[...] * _silu(g)
    y = jnp.dot(hg.astype(BF16), wout_ref[...], preferred_element_type=F32)
    y_ref[0] = x + _rms(y, post_ref[...])


def _layer_a(x, h0, c0, pre, post, win, cw, cb, wg, bga, bgx, lam, wout, *, tt, nb):
    g_, rows_total, d = x.shape
    rows = tt * nb
    nt = rows_total // rows
    assert nt * rows == rows_total and tt >= CONV_W - 1
    assert (nb == 1 and tt % SUBLANES == 0) or nb % SUBLANES == 0
    tail = (CONV_W - 1) * nb
    off = -(-tail // SUBLANES) * SUBLANES
    row = lambda v: v.reshape(1, d)
    kern = functools.partial(_layer_a_kernel, tt=tt, nb=nb, off=off)
    return pl.pallas_call(
        kern,
        grid=(g_, nt),
        in_specs=[
            pl.BlockSpec((1, rows, d), lambda g, t: (g, t, 0)),
            pl.BlockSpec((1, nb, d), lambda g, t: (g, 0, 0)),
            pl.BlockSpec((1, tail, d), lambda g, t: (g, 0, 0)),
            _const_spec((1, d)), _const_spec((1, d)),
            _const_spec(win.shape), _const_spec(cw.shape), _const_spec((1, d)),
            _const_spec(wg.shape), _const_spec((1, d)), _const_spec((1, d)), _const_spec((1, d)),
            _const_spec(wout.shape),
        ],
        out_specs=[
            pl.BlockSpec((1, rows, d), lambda g, t: (g, t, 0)),
            pl.BlockSpec((1, nb, d), lambda g, t: (g, 0, 0)),
            pl.BlockSpec((1, tail, d), lambda g, t: (g, 0, 0)),
        ],
        out_shape=[
            jax.ShapeDtypeStruct((g_, rows_total, d), F32),
            jax.ShapeDtypeStruct((g_, nb, d), F32),
            jax.ShapeDtypeStruct((g_, tail, d), F32),
        ],
        scratch_shapes=[
            pltpu.VMEM((off + rows, d), F32),
            pltpu.VMEM((rows, d), F32),
            pltpu.VMEM((rows, d), F32),
            pltpu.VMEM((nb, d), F32),
        ],
        compiler_params=_params("parallel", "arbitrary"),
        name="rglru_layer",
    )(x, h0, c0, row(pre), row(post), win, cw, row(cb), wg, row(bga), row(bgx), row(lam), wout)


def _aug_base(h, hd):
    upper = ((h * hd) % LANES) != 0
    return (0 if upper else hd), upper


def _key_block(k, col, h, hd):
    rows = k.shape[0]
    pair = (h * hd) // LANES
    blk = k[:, pair * LANES:(pair + 1) * LANES]
    a0, upper = _aug_base(h, hd)
    lane = lax.broadcasted_iota(jnp.int32, (rows, LANES), 1)
    data = (lane >= hd) if upper else (lane < hd)
    hi, mid, lo = _split3(-LOG2E * col)
    v0 = a0 + N_AUG
    aug = jnp.where(lane == v0, hi,
                    jnp.where(lane == v0 + 1, mid,
                              jnp.where(lane == v0 + 2, lo,
                                        jnp.where((lane >= a0) & (lane < v0), 1.0, 0.0))))
    return jnp.where(data, blk, aug).astype(BF16)


def _query_block_t(qt, row, h, hd):
    cols = qt.shape[1]
    a0, upper = _aug_base(h, hd)
    sub = lax.broadcasted_iota(jnp.int32, (SUBLANES, cols), 0)
    hi, mid, lo = _split3(LOG2E * row)
    aug = jnp.where(sub == 0, hi,
                    jnp.where(sub == 1, mid,
                              jnp.where(sub == 2, lo,
                                        jnp.where(sub < 2 * N_AUG, 1.0, 0.0))))
    pad = jnp.zeros((LANES - hd - SUBLANES, cols), F32)
    data = qt[h * hd:(h + 1) * hd, :]
    parts = [aug, pad, data] if upper else [data, aug, pad]
    return jnp.concatenate(parts, axis=0).astype(BF16)


def _kv_prompt_kernel(x_ref, g_ref, wk_ref, wkt_ref, wvt_ref, wf_ref, wft_ref, bfr_ref, bfc_ref,
                      kt_ref, vt_ref, lft_ref, ct_ref, kp_ref, vtb_ref, carry_r, carry_c,
                      *, n_heads):
    x = x_ref[0]
    tt, d = x.shape
    hd = d // n_heads
    xn = _rms(x, g_ref[...]).astype(BF16)
    kt = lax.dot_general(wkt_ref[...], xn, _NT, preferred_element_type=F32)
    vt = lax.dot_general(wvt_ref[...], xn, _NT, preferred_element_type=F32)
    k = jnp.dot(xn, wk_ref[...], preferred_element_type=F32)
    lf = _log_sigmoid(jnp.dot(xn, wf_ref[...], preferred_element_type=F32) + bfr_ref[...])
    lft = _log_sigmoid(lax.dot_general(wft_ref[...], xn, _NT, preferred_element_type=F32)
                       + bfc_ref[...])
    kt_ref[0] = kt
    vt_ref[0] = vt
    vtb_ref[0] = vt.astype(BF16)
    lft_ref[0] = lft

    @pl.when(pl.program_id(1) == 0)
    def _():
        carry_r[...] = jnp.zeros_like(carry_r)
        carry_c[...] = jnp.zeros_like(carry_c)

    r = lax.broadcasted_iota(jnp.int32, (tt, tt), 0)
    c = lax.broadcasted_iota(jnp.int32, (tt, tt), 1)
    cs = _exact_dot(jnp.where(r >= c, 1.0, 0.0).astype(BF16), lf) + carry_r[...]
    cst = _exact_dot_rhs(lft, jnp.where(r <= c, 1.0, 0.0).astype(BF16)) + carry_c[...]
    carry_r[...] = cs[tt - 1:tt, :]
    carry_c[...] = cst[:, tt - 1:tt]
    ct_ref[0] = cst
    for h in range(n_heads):
        kp_ref[0, h] = _key_block(k, cs[:, h:h + 1], h, hd)


def _kv_prompt(x, g, wk, wkt, wvt, wf, wft, bfr, bfc, *, tt, n_heads):
    b, t, d = x.shape
    nt = t // tt
    assert nt * tt == t
    feat = lambda n, dt: (pl.BlockSpec((1, n, tt), lambda i, j: (i, 0, j)),
                          jax.ShapeDtypeStruct((b, n, t), dt))
    outs = [feat(d, F32), feat(d, F32), feat(n_heads, F32), feat(n_heads, F32),
            (pl.BlockSpec((1, n_heads, tt, LANES), lambda i, j: (i, 0, j, 0)),
             jax.ShapeDtypeStruct((b, n_heads, t, LANES), BF16)),
            feat(d, BF16)]
    return pl.pallas_call(
        functools.partial(_kv_prompt_kernel, n_heads=n_heads),
        grid=(b, nt),
        in_specs=[pl.BlockSpec((1, tt, d), lambda i, j: (i, j, 0)), _const_spec((1, d)),
                  _const_spec(wk.shape), _const_spec(wkt.shape), _const_spec(wvt.shape),
                  _const_spec(wf.shape), _const_spec(wft.shape),
                  _const_spec(bfr.shape), _const_spec(bfc.shape)],
        out_specs=[o[0] for o in outs],
        out_shape=[o[1] for o in outs],
        scratch_shapes=[pltpu.VMEM((1, LANES), F32), pltpu.VMEM((n_heads, 1), F32)],
        compiler_params=_params("parallel", "arbitrary"),
        name="kv_prompt",
    )(x, g.reshape(1, d), wk, wkt, wvt, wf, wft, bfr, bfc)


def _qg_prompt_kernel(x_ref, g_ref, wt_ref, ct_ref, qt_ref, gt_ref, *, n_heads):
    x = x_ref[0]
    d = x.shape[-1]
    hd = d // n_heads
    xn = _rms(x, g_ref[...]).astype(BF16)
    zt = lax.dot_general(wt_ref[...], xn, _NT, preferred_element_type=F32)
    gt_ref[0] = zt[d:, :]
    qt = zt[:d, :] * (hd ** -0.5 * LOG2E)
    ct = ct_ref[0]
    for h in range(n_heads):
        qt_ref[0, h] = _query_block_t(qt, ct[h:h + 1, :], h, hd)


def _qg_prompt(x, g, wt, ct, *, tt, n_heads):
    b, t, d = x.shape
    nt = t // tt
    assert nt * tt == t
    return pl.pallas_call(
        functools.partial(_qg_prompt_kernel, n_heads=n_heads),
        grid=(b, nt),
        in_specs=[pl.BlockSpec((1, tt, d), lambda i, j: (i, j, 0)), _const_spec((1, d)),
                  _const_spec(wt.shape),
                  pl.BlockSpec((1, n_heads, tt), lambda i, j: (i, 0, j))],
        out_specs=[pl.BlockSpec((1, n_heads, LANES, tt), lambda i, j: (i, 0, 0, j)),
                   pl.BlockSpec((1, d, tt), lambda i, j: (i, 0, j))],
        out_shape=[jax.ShapeDtypeStruct((b, n_heads, LANES, t), BF16),
                   jax.ShapeDtypeStruct((b, d, t), F32)],
        compiler_params=_params("parallel", "parallel"),
        name="qg_prompt",
    )(x, g.reshape(1, d), wt, ct)


def _attn_kernel(qi_tab, ki_tab, q_ref, k_ref, v_ref, o_ref, m_s, l_s, al_s, acc_s, *bufs, hd):
    st_bufs, pt_bufs = bufs[:4], bufs[4:]
    step = pl.program_id(1)
    qi = qi_tab[step]
    ki = ki_tab[step]
    n_heads, _, tq = q_ref.shape[1:]
    tk = k_ref.shape[2]
    n_pairs = n_heads // 2
    assert n_heads % 4 == 0 and n_pairs >= 4

    @pl.when(ki == 0)
    def _():
        m_s[...] = jnp.full(m_s.shape, -jnp.inf, F32)
        l_s[...] = jnp.zeros_like(l_s)
        acc_s[...] = jnp.zeros_like(acc_s)

    def run(diag):
        def scores(pair, s):
            for j in range(2):
                h = 2 * pair + j
                st = jnp.dot(k_ref[0, h], q_ref[0, h], preferred_element_type=F32)
                if diag:
                    keep = (lax.broadcasted_iota(jnp.int32, (tk, tq), 0)
                            <= lax.broadcasted_iota(jnp.int32, (tk, tq), 1))
                    st = jnp.where(keep, st, -jnp.inf)
                st_bufs[2 * s + j][...] = st

        def softmax(pair, s):
            for j in range(2):
                h = 2 * pair + j
                st = st_bufs[2 * s + j][...]
                m_prev = m_s[h]
                m_new = jnp.maximum(m_prev, jnp.max(st, axis=0, keepdims=True))
                m_s[h] = m_new
                al_s[h] = jnp.exp2(m_prev - m_new)
                pt_bufs[2 * s + j][...] = jnp.exp2(st - m_new).astype(BF16)

        ones_rows = jnp.ones((BF16_ROWS, tk), BF16)

        def values(pair, s):
            for j in range(2):
                h = 2 * pair + j
                rows = pl.ds(h * hd if isinstance(h, int) else pl.multiple_of(h * hd, hd), hd)
                pv = jnp.dot(jnp.concatenate([v_ref[0, rows, :], ones_rows], axis=0),
                             pt_bufs[2 * s + j][...], preferred_element_type=F32)
                alpha = al_s[h]
                acc_s[rows, :] = acc_s[rows, :] * alpha + pv[:hd, :]
                l_s[h] = alpha * l_s[h] + pv[hd:hd + 1, :]

        scores(0, 0)
        scores(1, 1)
        softmax(0, 0)

        def two_slots(i, carry):
            t = 2 * i
            scores(t, 0)
            softmax(t - 1, 1)
            values(t - 2, 0)
            scores(t + 1, 1)
            softmax(t, 0)
            values(t - 1, 1)
            return carry

        lax.fori_loop(1, n_pairs // 2, two_slots, 0)
        softmax(n_pairs - 1, 1)
        values(n_pairs - 2, 0)
        values(n_pairs - 1, 1)

    @pl.when(ki < qi)
    def _():
        run(False)

    @pl.when(ki == qi)
    def _():
        run(True)
        for h in range(n_heads):
            rows = slice(h * hd, (h + 1) * hd)
            o_ref[0, rows, :] = acc_s[rows, :] * (1.0 / l_s[h])


def _prompt_attention(qt, kp, vtb, *, blk, hd):
    b, n_heads, _, t = qt.shape
    d = vtb.shape[1]
    nq = t // blk
    assert nq * blk == t
    qi_l, ki_l = [], []
    for i in range(nq):
        for j in range(i + 1):
            qi_l.append(i)
            ki_l.append(j)
    grid_spec = pltpu.PrefetchScalarGridSpec(
        num_scalar_prefetch=2,
        grid=(b, len(qi_l)),
        in_specs=[
            pl.BlockSpec((1, n_heads, LANES, blk), lambda i, s, qt_, kt_: (i, 0, 0, qt_[s])),
            pl.BlockSpec((1, n_heads, blk, LANES), lambda i, s, qt_, kt_: (i, 0, kt_[s], 0)),
            pl.BlockSpec((1, d, blk), lambda i, s, qt_, kt_: (i, 0, kt_[s])),
        ],
        out_specs=pl.BlockSpec((1, d, blk), lambda i, s, qt_, kt_: (i, 0, qt_[s])),
        scratch_shapes=[
            pltpu.VMEM((n_heads, 1, blk), F32),
            pltpu.VMEM((n_heads, 1, blk), F32),
            pltpu.VMEM((n_heads, 1, blk), F32),
            pltpu.VMEM((d, blk), F32),
        ] + [pltpu.VMEM((blk, blk), F32)] * 4 + [pltpu.VMEM((blk, blk), BF16)] * 4,
    )
    return pl.pallas_call(
        functools.partial(_attn_kernel, hd=hd),
        grid_spec=grid_spec,
        out_shape=jax.ShapeDtypeStruct((b, d, t), F32),
        compiler_params=_params("parallel", "arbitrary"),
        name="prompt_attention",
    )(jnp.asarray(qi_l, jnp.int32), jnp.asarray(ki_l, jnp.int32), qt, kp, vtb)


def _out_prompt_kernel(ot_ref, gt_ref, x_ref, w_ref, post_ref, y_ref):
    hgt = ot_ref[0] * _silu(gt_ref[0])
    y = lax.dot_general(hgt.astype(BF16), w_ref[...], _TN, preferred_element_type=F32)
    y_ref[0] = x_ref[0] + _rms(y, post_ref[...])


def _out_prompt(ot, gt, x, wt, post, *, tt):
    b, t, d = x.shape
    nt = t // tt
    assert nt * tt == t
    row_spec = pl.BlockSpec((1, tt, d), lambda i, j: (i, j, 0))
    feat_spec = pl.BlockSpec((1, d, tt), lambda i, j: (i, 0, j))
    return pl.pallas_call(
        _out_prompt_kernel,
        grid=(b, nt),
        in_specs=[feat_spec, feat_spec, row_spec, _const_spec(wt.shape), _const_spec((1, d))],
        out_specs=row_spec,
        out_shape=jax.ShapeDtypeStruct((b, t, d), F32),
        compiler_params=_params("parallel", "parallel"),
        name="out_prompt",
    )(ot, gt, x, wt, post.reshape(1, d))


def _kv_sample_kernel(x_ref, g_ref, wk_ref, wv_ref, wf_ref, bfr_ref, k_ref, v_ref, lf_ref,
                      *, n_heads):
    xn = _rms(x_ref[...], g_ref[...]).astype(BF16)
    k_ref[...] = jnp.dot(xn, wk_ref[...], preferred_element_type=F32)
    v_ref[...] = jnp.dot(xn, wv_ref[...], preferred_element_type=F32)
    lf = _log_sigmoid(jnp.dot(xn, wf_ref[...], preferred_element_type=F32) + bfr_ref[...])
    lf_ref[...] = lf[:, :n_heads]


def _kv_sample(x, g, wk, wv, wf, bfr, *, n_heads):
    rows, d = x.shape
    return pl.pallas_call(
        functools.partial(_kv_sample_kernel, n_heads=n_heads),
        out_shape=[jax.ShapeDtypeStruct((rows, d), F32), jax.ShapeDtypeStruct((rows, d), F32),
                   jax.ShapeDtypeStruct((rows, n_heads), F32)],
        compiler_params=pltpu.CompilerParams(vmem_limit_bytes=VMEM_LIMIT),
        name="kv_sample",
    )(x, g.reshape(1, d), wk, wv, wf, bfr)


def _qg_sample_kernel(x_ref, g_ref, w_ref, q_ref, gate_ref, *, n_heads):
    d = x_ref.shape[-1]
    xn = _rms(x_ref[...], g_ref[...]).astype(BF16)
    z = jnp.dot(xn, w_ref[...], preferred_element_type=F32)
    q_ref[...] = z[:, :d] * ((d // n_heads) ** -0.5)
    gate_ref[...] = z[:, d:]


def _qg_sample(x, g, w, *, n_heads):
    rows, d = x.shape
    return pl.pallas_call(
        functools.partial(_qg_sample_kernel, n_heads=n_heads),
        out_shape=[jax.ShapeDtypeStruct((rows, d), F32), jax.ShapeDtypeStruct((rows, d), F32)],
        compiler_params=pltpu.CompilerParams(vmem_limit_bytes=VMEM_LIMIT),
        name="qg_sample",
    )(x, g.reshape(1, d), w)


def _out_sample_kernel(o_ref, gate_ref, x_ref, w_ref, post_ref, y_ref):
    hg = o_ref[...] * _silu(gate_ref[...])
    y = jnp.dot(hg.astype(BF16), w_ref[...], preferred_element_type=F32)
    y_ref[...] = x_ref[...] + _rms(y, post_ref[...])


def _out_sample(o, gate, x, w, post):
    rows, d = x.shape
    return pl.pallas_call(
        _out_sample_kernel,
        out_shape=jax.ShapeDtypeStruct((rows, d), F32),
        compiler_params=pltpu.CompilerParams(vmem_limit_bytes=VMEM_LIMIT),
        name="out_sample",
    )(o, gate, x, w, post.reshape(1, d))


PAGE_GROUP = 4
RING_GROUPS = 5
N_SLOTS = RING_GROUPS * PAGE_GROUP


def _decode_kernel(pt_ref, q_ref, kn_ref, vn_ref, lfn_ref, kc_hbm, vc_hbm, lfc_hbm, o_ref,
                   kvbuf, lbuf, sem_kv, sem_l, s_ref, p_ref, acc_ref, qbd_ref, *, n_heads):
    seq = pl.program_id(0)
    n_seq = pl.num_programs(0)
    n_pages = pt_ref.shape[1]
    d, page = kvbuf.shape[1:]
    nq = q_ref.shape[1]
    hd = d // n_heads
    rows = n_heads * nq
    assert rows == LANES and page == LANES and n_pages % PAGE_GROUP == 0
    k_groups = n_pages // PAGE_GROUP
    n_groups = 2 * k_groups

    def kv_copy(src_hbm, page_id, slot):
        return pltpu.make_async_copy(src_hbm.at[page_id], kvbuf.at[slot], sem_kv.at[slot])

    def lf_copy(page_id, slot):
        return pltpu.make_async_copy(lfc_hbm.at[page_id], lbuf.at[slot], sem_l.at[slot])

    first = seq * n_groups

    def group_base(gid):
        return lax.rem(gid, RING_GROUPS) * PAGE_GROUP

    def start_group(gid):
        base = group_base(gid)
        sq = lax.div(gid, n_groups)
        grp = lax.rem(gid, n_groups)
        live = gid < n_seq * n_groups

        @pl.when(live & (grp < k_groups))
        def _():
            for j in range(PAGE_GROUP):
                pid = pt_ref[sq, grp * PAGE_GROUP + j]
                kv_copy(kc_hbm, pid, base + j).start()
                lf_copy(pid, base + j).start()

        @pl.when(live & (grp >= k_groups))
        def _():
            for j in range(PAGE_GROUP):
                pid = pt_ref[sq, (grp - k_groups) * PAGE_GROUP + j]
                kv_copy(vc_hbm, pid, base + j).start()

    @pl.when(seq == 0)
    def _():
        for g in range(RING_GROUPS):
            start_group(jnp.int32(g))

    q = q_ref[0]
    lane_head = lax.broadcasted_iota(jnp.int32, (nq, d), 1) // hd
    qbd_ref[...] = jnp.concatenate([jnp.where(lane_head == h, q, 0.0) for h in range(n_heads)],
                                   axis=0).astype(BF16)

    r = lax.broadcasted_iota(jnp.int32, (page, 2 * page), 0)
    c = lax.broadcasted_iota(jnp.int32, (page, 2 * page), 1)
    tri_ones = jnp.where((r <= c) | (c >= page), 1.0, 0.0).astype(BF16)

    def head_rows(ct):
        return jnp.concatenate(
            [jnp.broadcast_to(ct[h:h + 1, :], (nq, ct.shape[1])) for h in range(n_heads)], axis=0)

    def k_group(g, carry):
        gid = first + g
        base = group_base(gid)
        for j in range(PAGE_GROUP):
            kv_copy(kc_hbm, 0, base + j).wait()
            lf_copy(0, base + j).wait()
        for j in range(PAGE_GROUP):
            sc = jnp.dot(qbd_ref[...], kvbuf[base + j].astype(BF16), preferred_element_type=F32)
            cum = _exact_dot_rhs(lbuf[base + j], tri_ones)
            s_ref[g * PAGE_GROUP + j] = sc - head_rows(carry + cum[:, :page])
            carry = carry + cum[:, page:]
        start_group(gid + RING_GROUPS)
        return carry

    past_total = lax.fori_loop(0, k_groups, k_group, jnp.zeros((n_heads, page), F32))

    ct_new = past_total + _exact_dot_rhs(lfn_ref[0], tri_ones)[:, :page]
    kn = jnp.concatenate([kn_ref[0], jnp.zeros((page - nq, d), F32)], axis=0).astype(BF16)
    sc_new = lax.dot_general(qbd_ref[...], kn, _NT, preferred_element_type=F32)
    key_idx = lax.broadcasted_iota(jnp.int32, (rows, page), 1)
    q_idx = lax.rem(lax.broadcasted_iota(jnp.int32, (rows, page), 0), nq)
    ct_rows = head_rows(ct_new)
    cq = jnp.sum(jnp.where(key_idx == q_idx, ct_rows, 0.0), axis=1, keepdims=True)
    s_ref[n_pages] = jnp.where(key_idx <= q_idx, sc_new - ct_rows, -jnp.inf)

    def max_body(i, m):
        return jnp.maximum(m, s_ref[i])
    m_el = lax.fori_loop(0, n_pages + 1, max_body, jnp.full((rows, page), -jnp.inf, F32))
    m = jnp.max(m_el, axis=1, keepdims=True) + cq

    def p_body(i, l):
        pe = jnp.exp((s_ref[i] + cq) - m)
        p_ref[i] = pe.astype(BF16)
        return l + pe
    l_el = lax.fori_loop(0, n_pages + 1, p_body, jnp.zeros((rows, page), F32))
    l = jnp.sum(l_el, axis=1, keepdims=True)

    vn = jnp.concatenate([vn_ref[0], jnp.zeros((page - nq, d), F32)], axis=0).astype(BF16)
    acc_ref[...] = jnp.dot(p_ref[n_pages], vn, preferred_element_type=F32)

    def v_group(g, carry):
        gid = first + k_groups + g
        base = group_base(gid)
        for j in range(PAGE_GROUP):
            kv_copy(vc_hbm, 0, base + j).wait()
        upd = None
        for j in range(PAGE_GROUP):
            pv = lax.dot_general(p_ref[g * PAGE_GROUP + j], kvbuf[base + j].astype(BF16), _NT,
                                 preferred_element_type=F32)
            upd = pv if upd is None else upd + pv
        acc_ref[...] += upd
        start_group(gid + RING_GROUPS)
        return carry

    lax.fori_loop(0, k_groups, v_group, 0)

    acc = acc_ref[...] * (1.0 / l)
    out = jnp.zeros((nq, d), F32)
    for h in range(n_heads):
        out = out + jnp.where(lane_head == h, acc[h * nq:(h + 1) * nq, :], 0.0)
    o_ref[0] = out


def _decode_attention(page_table, q, k_new, v_new, lft_new, cache_kt, cache_vt, cache_lft, *, n_heads):
    b, nq, d = q.shape
    n_pages = page_table.shape[1]
    page = cache_kt.shape[2]
    rows = n_heads * nq
    seq_spec = pl.BlockSpec((1, nq, d), lambda i, pt: (i, 0, 0))
    grid_spec = pltpu.PrefetchScalarGridSpec(
        num_scalar_prefetch=1,
        grid=(b,),
        in_specs=[
            seq_spec, seq_spec, seq_spec,
            pl.BlockSpec((1, n_heads, page), lambda i, pt: (i, 0, 0)),
            pl.BlockSpec(memory_space=pl.ANY),
            pl.BlockSpec(memory_space=pl.ANY),
            pl.BlockSpec(memory_space=pl.ANY),
        ],
        out_specs=seq_spec,
        scratch_shapes=[
            pltpu.VMEM((N_SLOTS, d, page), F32),
            pltpu.VMEM((N_SLOTS, n_heads, page), F32),
            pltpu.SemaphoreType.DMA((N_SLOTS,)),
            pltpu.SemaphoreType.DMA((N_SLOTS,)),
            pltpu.VMEM((n_pages + 1, rows, page), F32),
            pltpu.VMEM((n_pages + 1, rows, page), BF16),
            pltpu.VMEM((rows, d), F32),
            pltpu.VMEM((rows, d), BF16),
        ],
    )
    return pl.pallas_call(
        functools.partial(_decode_kernel, n_heads=n_heads),
        grid_spec=grid_spec,
        out_shape=jax.ShapeDtypeStruct((b, nq, d), F32),
        compiler_params=_params("arbitrary"),
        name="decode_attention",
    )(page_table, q, k_new, v_new, lft_new, cache_kt, cache_vt, cache_lft)


def kernel(x_prompt, x_sample, cache_k, cache_v, cache_logf, state_rnn, state_conv, page_table,
           a_pre_norm, a_post_norm, a_w_in, a_conv_w, a_conv_b, a_w_ga, a_b_ga, a_w_gx, a_b_gx,
           a_lambda, a_w_out, kv_norm, w_kv, b_f, b_pre_norm, b_post_norm, b_w_in, b_w_out):
    n_a = a_w_in.shape[0]
    n_b = b_w_in.shape[0]
    d = x_prompt.shape[-1]
    n_heads = b_f.shape[0]
    hd = d // n_heads
    assert 2 * hd == LANES and n_heads <= LANES

    a_win = a_w_in.astype(BF16)
    a_wout = a_w_out.astype(BF16)
    a_wg = jnp.concatenate([a_w_ga, a_w_gx], axis=-1).astype(BF16)
    w_kv_t = w_kv.T.astype(BF16)
    wkt, wvt, wft = w_kv_t[:d], w_kv_t[d:2 * d], w_kv_t[2 * d:]
    wk = w_kv[:, :d].astype(BF16)
    wv = w_kv[:, d:2 * d].astype(BF16)
    wf = jnp.pad(w_kv[:, 2 * d:], ((0, 0), (0, LANES - n_heads))).astype(BF16)
    bfr = jnp.pad(b_f, (0, LANES - n_heads)).reshape(1, LANES)
    bfc = b_f.reshape(n_heads, 1)
    b_win = b_w_in.astype(BF16)
    b_wout = b_w_out.astype(BF16)
    b_win_t = jnp.swapaxes(b_w_in, 1, 2).astype(BF16)

    def a_layer(l, x, h0, c0, tt, nb):
        return _layer_a(x, h0, c0, a_pre_norm[l], a_post_norm[l], a_win[l], a_conv_w[l], a_conv_b[l],
                        a_wg[l], a_b_ga[l], a_b_gx[l], a_lambda[l], a_wout[l], tt=tt, nb=nb)

    bp, tp, _ = x_prompt.shape
    x = x_prompt
    rnn_p, conv_p = [], []
    for l in range(n_a):
        x, hl, cn = a_layer(l, x, jnp.zeros((bp, 1, d), F32), jnp.zeros((bp, CONV_W - 1, d), F32),
                            tt=256, nb=1)
        rnn_p.append(hl.reshape(bp, d))
        conv_p.append(cn)
    kt_p, vt_p, lft_p, ct_p, kaug_p, vtb_p = _kv_prompt(x, kv_norm, wk, wkt, wvt, wf, wft, bfr, bfc,
                                                        tt=512, n_heads=n_heads)
    for l in range(n_b):
        qt, gt = _qg_prompt(x, b_pre_norm[l], b_win_t[l], ct_p, tt=512, n_heads=n_heads)
        ot = _prompt_attention(qt, kaug_p, vtb_p, blk=512, hd=hd)
        x = _out_prompt(ot, gt, x, b_wout[l], b_post_norm[l], tt=512)
    y_prompt = x
    k_p = jnp.transpose(kt_p.reshape(bp, n_heads, hd, tp), (0, 3, 1, 2))
    v_p = jnp.transpose(vt_p.reshape(bp, n_heads, hd, tp), (0, 3, 1, 2))
    lf_p = jnp.swapaxes(lft_p, 1, 2)

    bs, ts, _ = x_sample.shape
    rows_s = bs * ts
    n_pool, page = cache_k.shape[:2]
    x = jnp.swapaxes(x_sample, 0, 1).reshape(1, rows_s, d)
    rnn_s, conv_s = [], []
    for l in range(n_a):
        c0 = jnp.swapaxes(state_conv[l], 0, 1).reshape(1, (CONV_W - 1) * bs, d)
        x, hl, cn = a_layer(l, x, state_rnn[l].reshape(1, bs, d), c0, tt=ts, nb=bs)
        rnn_s.append(hl.reshape(bs, d))
        conv_s.append(jnp.swapaxes(cn.reshape(CONV_W - 1, bs, d), 0, 1))
    x = jnp.swapaxes(x.reshape(ts, bs, d), 0, 1).reshape(rows_s, d)
    k_s, v_s, lf_s = _kv_sample(x, kv_norm, wk, wv, wf, bfr, n_heads=n_heads)
    lft_new = jnp.pad(jnp.swapaxes(lf_s.reshape(bs, ts, n_heads), 1, 2),
                      ((0, 0), (0, 0), (0, page - ts)))
    cache_kt = jnp.transpose(cache_k, (0, 2, 3, 1)).reshape(n_pool, d, page)
    cache_vt = jnp.transpose(cache_v, (0, 2, 3, 1)).reshape(n_pool, d, page)
    cache_lft = jnp.swapaxes(cache_logf, 1, 2)
    for l in range(n_b):
        q, gate = _qg_sample(x, b_pre_norm[l], b_win[l], n_heads=n_heads)
        o = _decode_attention(page_table, q.reshape(bs, ts, d), k_s.reshape(bs, ts, d),
                              v_s.reshape(bs, ts, d), lft_new, cache_kt, cache_vt, cache_lft,
                              n_heads=n_heads)
        x = _out_sample(o.reshape(rows_s, d), gate, x, b_wout[l], b_post_norm[l])
    y_sample = x.reshape(bs, ts, d)

    return (y_prompt, y_sample, k_p, v_p, lf_p,
            jnp.stack(rnn_p), jnp.stack(conv_p),
            k_s.reshape(bs, ts, n_heads, hd), v_s.reshape(bs, ts, n_heads, hd),
            lf_s.reshape(bs, ts, n_heads),
            jnp.stack(rnn_s), jnp.stack(conv_s))
```

```python
import functools

import jax
import jax.numpy as jnp
from jax import lax
from jax.experimental import pallas as pl
from jax.experimental.pallas import tpu as pltpu

F32 = jnp.float32
BF16 = jnp.bfloat16

EPS = 1e-6
RGLRU_C = 8.0
CONV_W = 4
LANES = 128
SUBLANES = 8
BF16_ROWS = 16
N_AUG = 3
LOG2E = 1.4426950408889634
VMEM_LIMIT = 56 * 1024 * 1024
RGLRU_TILE = 256
PROMPT_TILE = 512


def _tile(n, want):
    t = min(n, want)
    while n % t or t % LANES:
        t -= LANES
    return t

_NT = (((1,), (1,)), ((), ()))
_TN = (((0,), (0,)), ((), ()))


def _rms(x, g):
    ms = jnp.mean(x * x, axis=-1, keepdims=True)
    return x * lax.rsqrt(ms + EPS) * g


def _softplus(x):
    return jnp.maximum(x, 0.0) + jnp.log1p(jnp.exp(-jnp.abs(x)))


def _log_sigmoid(x):
    return jnp.minimum(x, 0.0) - jnp.log1p(jnp.exp(-jnp.abs(x)))


def _expm1_of_square(x, u):
    d = u - 1.0
    edge = (d == 0.0) | (d == -1.0)
    r = d * x / jnp.log(jnp.where(edge, 0.5, u))
    return jnp.where(d == 0.0, x, jnp.where(d == -1.0, -1.0, r))


def _silu(x):
    return x * jax.nn.sigmoid(x)


def _split3(x):
    hi = x.astype(BF16).astype(F32)
    r1 = x - hi
    mid = r1.astype(BF16).astype(F32)
    lo = (r1 - mid).astype(BF16).astype(F32)
    return hi, mid, lo


def _exact_dot(ones_lhs, x):
    hi, mid, lo = _split3(x)
    d = lambda p: jnp.dot(ones_lhs, p.astype(BF16), preferred_element_type=F32)
    return d(hi) + d(mid) + d(lo)


def _exact_dot_rhs(x, ones_rhs):
    hi, mid, lo = _split3(x)
    d = lambda p: jnp.dot(p.astype(BF16), ones_rhs, preferred_element_type=F32)
    return d(hi) + d(mid) + d(lo)


def _const_spec(shape):
    nd = len(shape)
    return pl.BlockSpec(shape, lambda *_: (0,) * nd)


def _params(*sem):
    return pltpu.CompilerParams(dimension_semantics=sem, vmem_limit_bytes=VMEM_LIMIT)


SEQS_PER_STEP = 2


def _rglru_gates(uc, wg_ref, bga_ref, bgx_ref, lam_ref, a_s, b_s):
    d = uc.shape[-1]
    nblk = wg_ref.shape[0]
    blk = d // nblk
    rate = (-RGLRU_C) * _softplus(-lam_ref[...])
    ucb = uc.astype(BF16)
    for n in range(nblk):
        sl = slice(n * blk, (n + 1) * blk)
        gz = jnp.dot(ucb[:, sl], wg_ref[n], preferred_element_type=F32)
        r = jax.nn.sigmoid(gz[:, :blk] + bga_ref[:, sl])
        i = jax.nn.sigmoid(gz[:, blk:] + bgx_ref[:, sl])
        log_a = r * rate[:, sl]
        a = jnp.exp(log_a)
        a_s[:, sl] = a
        b_s[:, sl] = jnp.sqrt(-_expm1_of_square(2.0 * log_a, a * a)) * (i * uc[:, sl])


def _layer_a_seq_kernel(x_ref, h0_ref, c0_ref, pre_ref, post_ref, win_ref, cw_ref, cb_ref,
                        wg_ref, bga_ref, bgx_ref, lam_ref, wout_ref,
                        y_ref, hl_ref, cn_ref,
                        ubuf, a_s, b_s, g_s, hcar, tail_s, perm_s, *, tt):
    t = pl.program_id(1)
    n_seq, _, d = x_ref.shape
    seg = tt // SUBLANES
    n_tail = CONV_W - 1
    head = n_tail * SUBLANES

    @pl.when(t == 0)
    def _():
        hcar[...] = h0_ref[...]
        tail_s[:, 0:n_tail, :] = c0_ref[...]
        r = lax.broadcasted_iota(jnp.int32, (tt, tt), 0)
        c = lax.broadcasted_iota(jnp.int32, (tt, tt), 1)
        time_of = lambda i: (i & (SUBLANES - 1)) * seg + (i >> 3)
        perm_s[0] = jnp.where(c == time_of(r), 1.0, 0.0).astype(BF16)
        perm_s[1] = jnp.where(r == time_of(c), 1.0, 0.0).astype(BF16)

    for b in range(n_seq):
        xn = _rms(x_ref[b], pre_ref[...]).astype(BF16)
        xp = jnp.dot(perm_s[0], xn, preferred_element_type=F32).astype(BF16)
        z = jnp.dot(xp, win_ref[...], preferred_element_type=F32)
        u = z[:, :d]
        g_s[b] = z[:, d:]

        ubuf[b, head:head + tt, :] = u
        sub = lax.broadcasted_iota(jnp.int32, (SUBLANES, d), 0)
        for j in range(n_tail):
            grp = u[(seg - n_tail + j) * SUBLANES:(seg - n_tail + j + 1) * SUBLANES, :]
            ubuf[b, j * SUBLANES:(j + 1) * SUBLANES, :] = jnp.where(
                sub == 0, tail_s[b, j:j + 1, :], pltpu.roll(grp, 1, 0))
            tail_s[b, j:j + 1, :] = grp[SUBLANES - 1:SUBLANES, :]
        cn_ref[b] = tail_s[b, 0:n_tail, :]

        cw = cw_ref[...]
        uc = u * cw[n_tail:CONV_W, :] + cb_ref[...]
        for j in range(n_tail):
            uc = uc + ubuf[b, j * SUBLANES:j * SUBLANES + tt, :] * cw[j:j + 1, :]

        _rglru_gates(uc, wg_ref, bga_ref, bgx_ref, lam_ref, a_s.at[b], b_s.at[b])

    def body(k, carry):
        rows = pl.ds(pl.multiple_of(k * SUBLANES, SUBLANES), SUBLANES)
        out = []
        for b in range(n_seq):
            h, prod = carry[b]
            a = a_s[b, rows, :]
            h = a * h + b_s[b, rows, :]
            prod = prod * a
            b_s[b, rows, :] = h
            a_s[b, rows, :] = prod
            out.append((h, prod))
        return tuple(out)
    init = (jnp.zeros((SUBLANES, d), F32), jnp.ones((SUBLANES, d), F32))
    ends = lax.fori_loop(0, seg, body, (init,) * n_seq)

    for b in range(n_seq):
        h_end, a_end = ends[b]
        c = hcar[b]
        enter = []
        for s in range(SUBLANES):
            enter.append(c)
            c = h_end[s:s + 1, :] + a_end[s:s + 1, :] * c
        hcar[b] = c
        hl_ref[b] = c
        enter = jnp.concatenate(enter, axis=0)[None]

        h = (b_s[b].reshape(seg, SUBLANES, d) + a_s[b].reshape(seg, SUBLANES, d) * enter)
        hg = (h.reshape(tt, d) * _silu(g_s[b])).astype(BF16)
        hg = jnp.dot(perm_s[1], hg, preferred_element_type=F32).astype(BF16)
        y = jnp.dot(hg, wout_ref[...], preferred_element_type=F32)
        y_ref[b] = x_ref[b] + _rms(y, post_ref[...])


def _layer_a_kernel(x_ref, h0_ref, c0_ref, pre_ref, post_ref, win_ref, cw_ref, cb_ref,
                    wg_ref, bga_ref, bgx_ref, lam_ref, wout_ref,
                    y_ref, hl_ref, cn_ref,
                    ubuf, a_s, b_s, hcar, *, tt, nb, off):
    t = pl.program_id(1)
    rows = tt * nb
    d = x_ref.shape[-1]
    tail = (CONV_W - 1) * nb

    @pl.when(t == 0)
    def _():
        hcar[...] = h0_ref[0]
        ubuf[off - tail:off, :] = c0_ref[0]

    x = x_ref[0]
    xn = _rms(x, pre_ref[...])
    z = jnp.dot(xn.astype(BF16), win_ref[...], preferred_element_type=F32)
    u = z[:, :d]
    g = z[:, d:]

    ubuf[off:off + rows, :] = u
    cw = cw_ref[...]
    uc = u * cw[CONV_W - 1:CONV_W, :] + cb_ref[...]
    for j in range(CONV_W - 1):
        lo = off - (CONV_W - 1 - j) * nb
        uc = uc + ubuf[lo:lo + rows, :] * cw[j:j + 1, :]
    ubuf[off - tail:off, :] = ubuf[off + rows - tail:off + rows, :]

    _rglru_gates(uc, wg_ref, bga_ref, bgx_ref, lam_ref, a_s, b_s)

    h = hcar[...]
    for s in range(tt):
        sl = slice(s * nb, (s + 1) * nb)
        h = a_s[sl, :] * h + b_s[sl, :]
        b_s[sl, :] = h
    hcar[...] = h
    hl_ref[0] = h
    cn_ref[0] = ubuf[off - tail:off, :]

    hg = b_s[...] * _silu(g)
    y = jnp.dot(hg.astype(BF16), wout_ref[...], preferred_element_type=F32)
    y_ref[0] = x + _rms(y, post_ref[...])


def _layer_a(x, h0, c0, pre, post, win, cw, cb, wg, bga, bgx, lam, wout, *, tt, nb):
    g_, rows_total, d = x.shape
    rows = tt * nb
    nt = rows_total // rows
    assert nt * rows == rows_total and tt >= CONV_W - 1
    assert (nb == 1 and tt % SUBLANES == 0) or nb % SUBLANES == 0
    tail = (CONV_W - 1) * nb
    off = -(-tail // SUBLANES) * SUBLANES
    row = lambda v: v.reshape(1, d)
    if nb == 1:
        assert tt % (SUBLANES * SUBLANES) == 0 and tt // SUBLANES > CONV_W - 1
        sps = SEQS_PER_STEP if g_ % SEQS_PER_STEP == 0 else 1
        kern = functools.partial(_layer_a_seq_kernel, tt=tt)
        scratch = [
            pltpu.VMEM((sps, tail * SUBLANES + rows, d), F32),
            pltpu.VMEM((sps, rows, d), F32),
            pltpu.VMEM((sps, rows, d), F32),
            pltpu.VMEM((sps, rows, d), F32),
            pltpu.VMEM((sps, 1, d), F32),
            pltpu.VMEM((sps, SUBLANES, d), F32),
            pltpu.VMEM((2, tt, tt), BF16),
        ]
    else:
        sps = 1
        kern = functools.partial(_layer_a_kernel, tt=tt, nb=nb, off=off)
        scratch = [
            pltpu.VMEM((off + rows, d), F32),
            pltpu.VMEM((rows, d), F32),
            pltpu.VMEM((rows, d), F32),
            pltpu.VMEM((nb, d), F32),
        ]
    return pl.pallas_call(
        kern,
        grid=(g_ // sps, nt),
        in_specs=[
            pl.BlockSpec((sps, rows, d), lambda g, t: (g, t, 0)),
            pl.BlockSpec((sps, nb, d), lambda g, t: (g, 0, 0)),
            pl.BlockSpec((sps, tail, d), lambda g, t: (g, 0, 0)),
            _const_spec((1, d)), _const_spec((1, d)),
            _const_spec(win.shape), _const_spec(cw.shape), _const_spec((1, d)),
            _const_spec(wg.shape), _const_spec((1, d)), _const_spec((1, d)), _const_spec((1, d)),
            _const_spec(wout.shape),
        ],
        out_specs=[
            pl.BlockSpec((sps, rows, d), lambda g, t: (g, t, 0)),
            pl.BlockSpec((sps, nb, d), lambda g, t: (g, 0, 0)),
            pl.BlockSpec((sps, tail, d), lambda g, t: (g, 0, 0)),
        ],
        out_shape=[
            jax.ShapeDtypeStruct((g_, rows_total, d), F32),
            jax.ShapeDtypeStruct((g_, nb, d), F32),
            jax.ShapeDtypeStruct((g_, tail, d), F32),
        ],
        scratch_shapes=scratch,
        compiler_params=_params("parallel", "arbitrary"),
        name="rglru_layer",
    )(x, h0, c0, row(pre), row(post), win, cw, row(cb), wg, row(bga), row(bgx), row(lam), wout)


def _aug_base(h, hd):
    upper = ((h * hd) % LANES) != 0
    return (0 if upper else hd), upper


def _key_block(k, col, h, hd):
    rows = k.shape[0]
    pair = (h * hd) // LANES
    blk = k[:, pair * LANES:(pair + 1) * LANES]
    a0, upper = _aug_base(h, hd)
    lane = lax.broadcasted_iota(jnp.int32, (rows, LANES), 1)
    data = (lane >= hd) if upper else (lane < hd)
    hi, mid, lo = _split3(-LOG2E * col)
    v0 = a0 + N_AUG
    aug = jnp.where(lane == v0, hi,
                    jnp.where(lane == v0 + 1, mid,
                              jnp.where(lane == v0 + 2, lo,
                                        jnp.where((lane >= a0) & (lane < v0), 1.0, 0.0))))
    return jnp.where(data, blk, aug).astype(BF16)


def _query_block_t(qt, row, h, hd):
    cols = qt.shape[1]
    a0, upper = _aug_base(h, hd)
    sub = lax.broadcasted_iota(jnp.int32, (SUBLANES, cols), 0)
    hi, mid, lo = _split3(LOG2E * row)
    aug = jnp.where(sub == 0, hi,
                    jnp.where(sub == 1, mid,
                              jnp.where(sub == 2, lo,
                                        jnp.where(sub < 2 * N_AUG, 1.0, 0.0))))
    pad = jnp.zeros((LANES - hd - SUBLANES, cols), F32)
    data = qt[h * hd:(h + 1) * hd, :]
    parts = [aug, pad, data] if upper else [data, aug, pad]
    return jnp.concatenate(parts, axis=0).astype(BF16)


def _kv_prompt_kernel(x_ref, g_ref, wk_ref, wkt_ref, wvt_ref, wf_ref, wft_ref, bfr_ref, bfc_ref,
                      kt_ref, vt_ref, lft_ref, ct_ref, kp_ref, vtb_ref, carry_r, carry_c,
                      *, n_heads):
    x = x_ref[0]
    tt, d = x.shape
    hd = d // n_heads
    xn = _rms(x, g_ref[...]).astype(BF16)
    kt = lax.dot_general(wkt_ref[...], xn, _NT, preferred_element_type=F32)
    vt = lax.dot_general(wvt_ref[...], xn, _NT, preferred_element_type=F32)
    k = jnp.dot(xn, wk_ref[...], preferred_element_type=F32)
    lf = _log_sigmoid(jnp.dot(xn, wf_ref[...], preferred_element_type=F32) + bfr_ref[...])
    lft = _log_sigmoid(lax.dot_general(wft_ref[...], xn, _NT, preferred_element_type=F32)
                       + bfc_ref[...])
    kt_ref[0] = kt
    vt_ref[0] = vt
    vtb_ref[0] = vt.astype(BF16)
    lft_ref[0] = lft

    @pl.when(pl.program_id(1) == 0)
    def _():
        carry_r[...] = jnp.zeros_like(carry_r)
        carry_c[...] = jnp.zeros_like(carry_c)

    r = lax.broadcasted_iota(jnp.int32, (tt, tt), 0)
    c = lax.broadcasted_iota(jnp.int32, (tt, tt), 1)
    cs = _exact_dot(jnp.where(r >= c, 1.0, 0.0).astype(BF16), lf) + carry_r[...]
    cst = _exact_dot_rhs(lft, jnp.where(r <= c, 1.0, 0.0).astype(BF16)) + carry_c[...]
    carry_r[...] = cs[tt - 1:tt, :]
    carry_c[...] = cst[:, tt - 1:tt]
    ct_ref[0] = cst
    for h in range(n_heads):
        kp_ref[0, h] = _key_block(k, cs[:, h:h + 1], h, hd)


def _kv_prompt(x, g, wk, wkt, wvt, wf, wft, bfr, bfc, *, tt, n_heads):
    b, t, d = x.shape
    nt = t // tt
    assert nt * tt == t
    feat = lambda n, dt: (pl.BlockSpec((1, n, tt), lambda i, j: (i, 0, j)),
                          jax.ShapeDtypeStruct((b, n, t), dt))
    outs = [feat(d, F32), feat(d, F32), feat(n_heads, F32), feat(n_heads, F32),
            (pl.BlockSpec((1, n_heads, tt, LANES), lambda i, j: (i, 0, j, 0)),
             jax.ShapeDtypeStruct((b, n_heads, t, LANES), BF16)),
            feat(d, BF16)]
    return pl.pallas_call(
        functools.partial(_kv_prompt_kernel, n_heads=n_heads),
        grid=(b, nt),
        in_specs=[pl.BlockSpec((1, tt, d), lambda i, j: (i, j, 0)), _const_spec((1, d)),
                  _const_spec(wk.shape), _const_spec(wkt.shape), _const_spec(wvt.shape),
                  _const_spec(wf.shape), _const_spec(wft.shape),
                  _const_spec(bfr.shape), _const_spec(bfc.shape)],
        out_specs=[o[0] for o in outs],
        out_shape=[o[1] for o in outs],
        scratch_shapes=[pltpu.VMEM((1, LANES), F32), pltpu.VMEM((n_heads, 1), F32)],
        compiler_params=_params("parallel", "arbitrary"),
        name="kv_prompt",
    )(x, g.reshape(1, d), wk, wkt, wvt, wf, wft, bfr, bfc)


def _qg_prompt_kernel(x_ref, g_ref, wt_ref, ct_ref, qt_ref, gt_ref, *, n_heads):
    x = x_ref[0]
    d = x.shape[-1]
    hd = d // n_heads
    xn = _rms(x, g_ref[...]).astype(BF16)
    zt = lax.dot_general(wt_ref[...], xn, _NT, preferred_element_type=F32)
    gt_ref[0] = zt[d:, :].astype(BF16)
    qt = zt[:d, :] * (hd ** -0.5 * LOG2E)
    ct = ct_ref[0]
    for h in range(n_heads):
        qt_ref[0, h] = _query_block_t(qt, ct[h:h + 1, :], h, hd)


def _qg_prompt(x, g, wt, ct, *, tt, n_heads):
    b, t, d = x.shape
    nt = t // tt
    assert nt * tt == t
    return pl.pallas_call(
        functools.partial(_qg_prompt_kernel, n_heads=n_heads),
        grid=(b, nt),
        in_specs=[pl.BlockSpec((1, tt, d), lambda i, j: (i, j, 0)), _const_spec((1, d)),
                  _const_spec(wt.shape),
                  pl.BlockSpec((1, n_heads, tt), lambda i, j: (i, 0, j))],
        out_specs=[pl.BlockSpec((1, n_heads, LANES, tt), lambda i, j: (i, 0, 0, j)),
                   pl.BlockSpec((1, d, tt), lambda i, j: (i, 0, j))],
        out_shape=[jax.ShapeDtypeStruct((b, n_heads, LANES, t), BF16),
                   jax.ShapeDtypeStruct((b, d, t), BF16)],
        compiler_params=_params("parallel", "parallel"),
        name="qg_prompt",
    )(x, g.reshape(1, d), wt, ct)


def _attn_kernel(qi_tab, ki_tab, q_ref, k_ref, v_ref, o_ref, m_s, l_s, al_s, mc_s, acc_s, *bufs, hd):
    st_bufs, pt_bufs = bufs[:4], bufs[4:]
    step = pl.program_id(1)
    qi = qi_tab[step]
    ki = ki_tab[step]
    n_heads, _, tq = q_ref.shape[1:]
    tk = k_ref.shape[2]
    n_pairs = n_heads // 2
    assert n_heads % 4 == 0 and n_pairs >= 4

    @pl.when(ki == 0)
    def _():
        m_s[...] = jnp.full(m_s.shape, -jnp.inf, F32)
        l_s[...] = jnp.zeros_like(l_s)
        acc_s[...] = jnp.zeros_like(acc_s)

    def run(diag):
        def scores(pair, s):
            for j in range(2):
                h = 2 * pair + j
                st = jnp.dot(k_ref[0, h], q_ref[0, h], preferred_element_type=F32)
                if diag:
                    keep = (lax.broadcasted_iota(jnp.int32, (tk, tq), 0)
                            <= lax.broadcasted_iota(jnp.int32, (tk, tq), 1))
                    st = jnp.where(keep, st, -jnp.inf)
                st_bufs[2 * s + j][...] = st
                mc_s[2 * s + j] = jnp.max(st, axis=0, keepdims=True)

        def softmax(pair, s):
            for j in range(2):
                h = 2 * pair + j
                m_prev = m_s[h]
                m_new = jnp.maximum(m_prev, mc_s[2 * s + j])
                m_s[h] = m_new
                al_s[h] = jnp.exp2(m_prev - m_new)
                pt_bufs[2 * s + j][...] = jnp.exp2(st_bufs[2 * s + j][...] - m_new).astype(BF16)

        ones_rows = jnp.ones((BF16_ROWS, tk), BF16)

        def values(pair, s):
            for j in range(2):
                h = 2 * pair + j
                rows = pl.ds(h * hd if isinstance(h, int) else pl.multiple_of(h * hd, hd), hd)
                pv = jnp.dot(jnp.concatenate([v_ref[0, rows, :], ones_rows], axis=0),
                             pt_bufs[2 * s + j][...], preferred_element_type=F32)
                alpha = al_s[h]
                acc_s[rows, :] = acc_s[rows, :] * alpha + pv[:hd, :]
                l_s[h] = alpha * l_s[h] + pv[hd:hd + 1, :]

        scores(0, 0)
        scores(1, 1)
        softmax(0, 0)

        def two_slots(i, carry):
            t = 2 * i
            scores(t, 0)
            softmax(t - 1, 1)
            values(t - 2, 0)
            scores(t + 1, 1)
            softmax(t, 0)
            values(t - 1, 1)
            return carry

        lax.fori_loop(1, n_pairs // 2, two_slots, 0)
        softmax(n_pairs - 1, 1)
        values(n_pairs - 2, 0)
        values(n_pairs - 1, 1)

    @pl.when(ki < qi)
    def _():
        run(False)

    @pl.when(ki == qi)
    def _():
        run(True)
        for h in range(n_heads):
            rows = slice(h * hd, (h + 1) * hd)
            o_ref[0, rows, :] = (acc_s[rows, :] * (1.0 / l_s[h])).astype(o_ref.dtype)


def _prompt_attention(qt, kp, vtb, *, blk, hd):
    b, n_heads, _, t = qt.shape
    d = vtb.shape[1]
    nq = t // blk
    assert nq * blk == t
    qi_l, ki_l = [], []
    for i in range(nq):
        for j in range(i + 1):
            qi_l.append(i)
            ki_l.append(j)
    grid_spec = pltpu.PrefetchScalarGridSpec(
        num_scalar_prefetch=2,
        grid=(b, len(qi_l)),
        in_specs=[
            pl.BlockSpec((1, n_heads, LANES, blk), lambda i, s, qt_, kt_: (i, 0, 0, qt_[s])),
            pl.BlockSpec((1, n_heads, blk, LANES), lambda i, s, qt_, kt_: (i, 0, kt_[s], 0)),
            pl.BlockSpec((1, d, blk), lambda i, s, qt_, kt_: (i, 0, kt_[s])),
        ],
        out_specs=pl.BlockSpec((1, d, blk), lambda i, s, qt_, kt_: (i, 0, qt_[s])),
        scratch_shapes=[
            pltpu.VMEM((n_heads, 1, blk), F32),
            pltpu.VMEM((n_heads, 1, blk), F32),
            pltpu.VMEM((n_heads, 1, blk), F32),
            pltpu.VMEM((4, 1, blk), F32),
            pltpu.VMEM((d, blk), F32),
        ] + [pltpu.VMEM((blk, blk), F32)] * 4 + [pltpu.VMEM((blk, blk), BF16)] * 4,
    )
    return pl.pallas_call(
        functools.partial(_attn_kernel, hd=hd),
        grid_spec=grid_spec,
        out_shape=jax.ShapeDtypeStruct((b, d, t), BF16),
        compiler_params=_params("parallel", "arbitrary"),
        name="prompt_attention",
    )(jnp.asarray(qi_l, jnp.int32), jnp.asarray(ki_l, jnp.int32), qt, kp, vtb)


def _out_prompt_kernel(ot_ref, gt_ref, x_ref, w_ref, post_ref, y_ref):
    hgt = ot_ref[0].astype(F32) * _silu(gt_ref[0].astype(F32))
    y = lax.dot_general(hgt.astype(BF16), w_ref[...], _TN, preferred_element_type=F32)
    y_ref[0] = x_ref[0] + _rms(y, post_ref[...])


def _out_prompt(ot, gt, x, wt, post, *, tt):
    b, t, d = x.shape
    nt = t // tt
    assert nt * tt == t
    row_spec = pl.BlockSpec((1, tt, d), lambda i, j: (i, j, 0))
    feat_spec = pl.BlockSpec((1, d, tt), lambda i, j: (i, 0, j))
    return pl.pallas_call(
        _out_prompt_kernel,
        grid=(b, nt),
        in_specs=[feat_spec, feat_spec, row_spec, _const_spec(wt.shape), _const_spec((1, d))],
        out_specs=row_spec,
        out_shape=jax.ShapeDtypeStruct((b, t, d), F32),
        compiler_params=_params("parallel", "parallel"),
        name="out_prompt",
    )(ot, gt, x, wt, post.reshape(1, d))


def _kv_sample_kernel(x_ref, g_ref, wk_ref, wv_ref, wf_ref, bfr_ref, k_ref, v_ref, lf_ref,
                      *, n_heads):
    xn = _rms(x_ref[...], g_ref[...]).astype(BF16)
    k_ref[...] = jnp.dot(xn, wk_ref[...], preferred_element_type=F32)
    v_ref[...] = jnp.dot(xn, wv_ref[...], preferred_element_type=F32)
    lf = _log_sigmoid(jnp.dot(xn, wf_ref[...], preferred_element_type=F32) + bfr_ref[...])
    lf_ref[...] = lf[:, :n_heads]


def _kv_sample(x, g, wk, wv, wf, bfr, *, n_heads):
    rows, d = x.shape
    return pl.pallas_call(
        functools.partial(_kv_sample_kernel, n_heads=n_heads),
        out_shape=[jax.ShapeDtypeStruct((rows, d), F32), jax.ShapeDtypeStruct((rows, d), F32),
                   jax.ShapeDtypeStruct((rows, n_heads), F32)],
        compiler_params=pltpu.CompilerParams(vmem_limit_bytes=VMEM_LIMIT),
        name="kv_sample",
    )(x, g.reshape(1, d), wk, wv, wf, bfr)


def _qg_sample_kernel(x_ref, g_ref, w_ref, q_ref, gate_ref, *, n_heads):
    d = x_ref.shape[-1]
    xn = _rms(x_ref[...], g_ref[...]).astype(BF16)
    z = jnp.dot(xn, w_ref[...], preferred_element_type=F32)
    q_ref[...] = z[:, :d] * ((d // n_heads) ** -0.5)
    gate_ref[...] = z[:, d:]


def _qg_sample(x, g, w, *, n_heads):
    rows, d = x.shape
    return pl.pallas_call(
        functools.partial(_qg_sample_kernel, n_heads=n_heads),
        out_shape=[jax.ShapeDtypeStruct((rows, d), F32), jax.ShapeDtypeStruct((rows, d), F32)],
        compiler_params=pltpu.CompilerParams(vmem_limit_bytes=VMEM_LIMIT),
        name="qg_sample",
    )(x, g.reshape(1, d), w)


def _out_sample_kernel(o_ref, gate_ref, x_ref, w_ref, post_ref, y_ref):
    hg = o_ref[...] * _silu(gate_ref[...])
    y = jnp.dot(hg.astype(BF16), w_ref[...], preferred_element_type=F32)
    y_ref[...] = x_ref[...] + _rms(y, post_ref[...])


def _out_sample(o, gate, x, w, post):
    rows, d = x.shape
    return pl.pallas_call(
        _out_sample_kernel,
        out_shape=jax.ShapeDtypeStruct((rows, d), F32),
        compiler_params=pltpu.CompilerParams(vmem_limit_bytes=VMEM_LIMIT),
        name="out_sample",
    )(o, gate, x, w, post.reshape(1, d))


PAGE_GROUP = 4
RING_GROUPS = 5
N_SLOTS = RING_GROUPS * PAGE_GROUP


def _decode_kernel(pt_ref, q_ref, kn_ref, vn_ref, lfn_ref, kc_hbm, vc_hbm, lfc_hbm, o_ref,
                   kvbuf, lbuf, sem_kv, sem_l, s_ref, p_ref, acc_ref, qbd_ref, *, n_heads):
    seq = pl.program_id(0)
    n_seq = pl.num_programs(0)
    n_pages = pt_ref.shape[1]
    d, page = kvbuf.shape[1:]
    nq = q_ref.shape[1]
    hd = d // n_heads
    rows = n_heads * nq
    assert rows == LANES and page == LANES and n_pages % PAGE_GROUP == 0
    k_groups = n_pages // PAGE_GROUP
    n_groups = 2 * k_groups

    def kv_copy(src_hbm, page_id, slot):
        return pltpu.make_async_copy(src_hbm.at[page_id], kvbuf.at[slot], sem_kv.at[slot])

    def lf_copy(page_id, slot):
        return pltpu.make_async_copy(lfc_hbm.at[page_id], lbuf.at[slot], sem_l.at[slot])

    first = seq * n_groups

    def group_base(gid):
        return lax.rem(gid, RING_GROUPS) * PAGE_GROUP

    def start_group(gid):
        base = group_base(gid)
        sq = lax.div(gid, n_groups)
        grp = lax.rem(gid, n_groups)
        live = gid < n_seq * n_groups

        @pl.when(live & (grp < k_groups))
        def _():
            for j in range(PAGE_GROUP):
                pid = pt_ref[sq, grp * PAGE_GROUP + j]
                kv_copy(kc_hbm, pid, base + j).start()
                lf_copy(pid, base + j).start()

        @pl.when(live & (grp >= k_groups))
        def _():
            for j in range(PAGE_GROUP):
                pid = pt_ref[sq, (grp - k_groups) * PAGE_GROUP + j]
                kv_copy(vc_hbm, pid, base + j).start()

    @pl.when(seq == 0)
    def _():
        for g in range(RING_GROUPS):
            start_group(jnp.int32(g))

    q = q_ref[0]
    lane_head = lax.broadcasted_iota(jnp.int32, (nq, d), 1) // hd
    qbd_ref[...] = jnp.concatenate([jnp.where(lane_head == h, q, 0.0) for h in range(n_heads)],
                                   axis=0).astype(BF16)

    r = lax.broadcasted_iota(jnp.int32, (page, 2 * page), 0)
    c = lax.broadcasted_iota(jnp.int32, (page, 2 * page), 1)
    tri_ones = jnp.where((r <= c) | (c >= page), 1.0, 0.0).astype(BF16)

    def head_rows(ct):
        return jnp.concatenate(
            [jnp.broadcast_to(ct[h:h + 1, :], (nq, ct.shape[1])) for h in range(n_heads)], axis=0)

    def k_group(g, carry):
        gid = first + g
        base = group_base(gid)
        for j in range(PAGE_GROUP):
            kv_copy(kc_hbm, 0, base + j).wait()
            lf_copy(0, base + j).wait()
        for j in range(PAGE_GROUP):
            sc = jnp.dot(qbd_ref[...], kvbuf[base + j].astype(BF16), preferred_element_type=F32)
            cum = _exact_dot_rhs(lbuf[base + j], tri_ones)
            s_ref[g * PAGE_GROUP + j] = sc - head_rows(carry + cum[:, :page])
            carry = carry + cum[:, page:]
        start_group(gid + RING_GROUPS)
        return carry

    past_total = lax.fori_loop(0, k_groups, k_group, jnp.zeros((n_heads, page), F32))

    ct_new = past_total + _exact_dot_rhs(lfn_ref[0], tri_ones)[:, :page]
    kn = jnp.concatenate([kn_ref[0], jnp.zeros((page - nq, d), F32)], axis=0).astype(BF16)
    sc_new = lax.dot_general(qbd_ref[...], kn, _NT, preferred_element_type=F32)
    key_idx = lax.broadcasted_iota(jnp.int32, (rows, page), 1)
    q_idx = lax.rem(lax.broadcasted_iota(jnp.int32, (rows, page), 0), nq)
    ct_rows = head_rows(ct_new)
    cq = jnp.sum(jnp.where(key_idx == q_idx, ct_rows, 0.0), axis=1, keepdims=True)
    s_ref[n_pages] = jnp.where(key_idx <= q_idx, sc_new - ct_rows, -jnp.inf)

    def max_body(i, m):
        return jnp.maximum(m, s_ref[i])
    m_el = lax.fori_loop(0, n_pages + 1, max_body, jnp.full((rows, page), -jnp.inf, F32))
    m = jnp.max(m_el, axis=1, keepdims=True) + cq

    def p_body(i, l):
        pe = jnp.exp((s_ref[i] + cq) - m)
        p_ref[i] = pe.astype(BF16)
        return l + pe
    l_el = lax.fori_loop(0, n_pages + 1, p_body, jnp.zeros((rows, page), F32))
    l = jnp.sum(l_el, axis=1, keepdims=True)

    vn = jnp.concatenate([vn_ref[0], jnp.zeros((page - nq, d), F32)], axis=0).astype(BF16)
    acc_ref[...] = jnp.dot(p_ref[n_pages], vn, preferred_element_type=F32)

    def v_group(g, carry):
        gid = first + k_groups + g
        base = group_base(gid)
        for j in range(PAGE_GROUP):
            kv_copy(vc_hbm, 0, base + j).wait()
        upd = None
        for j in range(PAGE_GROUP):
            pv = lax.dot_general(p_ref[g * PAGE_GROUP + j], kvbuf[base + j].astype(BF16), _NT,
                                 preferred_element_type=F32)
            upd = pv if upd is None else upd + pv
        acc_ref[...] += upd
        start_group(gid + RING_GROUPS)
        return carry

    lax.fori_loop(0, k_groups, v_group, 0)

    acc = acc_ref[...] * (1.0 / l)
    out = jnp.zeros((nq, d), F32)
    for h in range(n_heads):
        out = out + jnp.where(lane_head == h, acc[h * nq:(h + 1) * nq, :], 0.0)
    o_ref[0] = out


def _decode_attention(page_table, q, k_new, v_new, lft_new, cache_kt, cache_vt, cache_lft, *, n_heads):
    b, nq, d = q.shape
    n_pages = page_table.shape[1]
    page = cache_kt.shape[2]
    rows = n_heads * nq
    seq_spec = pl.BlockSpec((1, nq, d), lambda i, pt: (i, 0, 0))
    grid_spec = pltpu.PrefetchScalarGridSpec(
        num_scalar_prefetch=1,
        grid=(b,),
        in_specs=[
            seq_spec, seq_spec, seq_spec,
            pl.BlockSpec((1, n_heads, page), lambda i, pt: (i, 0, 0)),
            pl.BlockSpec(memory_space=pl.ANY),
            pl.BlockSpec(memory_space=pl.ANY),
            pl.BlockSpec(memory_space=pl.ANY),
        ],
        out_specs=seq_spec,
        scratch_shapes=[
            pltpu.VMEM((N_SLOTS, d, page), F32),
            pltpu.VMEM((N_SLOTS, n_heads, page), F32),
            pltpu.SemaphoreType.DMA((N_SLOTS,)),
            pltpu.SemaphoreType.DMA((N_SLOTS,)),
            pltpu.VMEM((n_pages + 1, rows, page), F32),
            pltpu.VMEM((n_pages + 1, rows, page), BF16),
            pltpu.VMEM((rows, d), F32),
            pltpu.VMEM((rows, d), BF16),
        ],
    )
    return pl.pallas_call(
        functools.partial(_decode_kernel, n_heads=n_heads),
        grid_spec=grid_spec,
        out_shape=jax.ShapeDtypeStruct((b, nq, d), F32),
        compiler_params=_params("arbitrary"),
        name="decode_attention",
    )(page_table, q, k_new, v_new, lft_new, cache_kt, cache_vt, cache_lft)


def kernel(x_prompt, x_sample, cache_k, cache_v, cache_logf, state_rnn, state_conv, page_table,
           a_pre_norm, a_post_norm, a_w_in, a_conv_w, a_conv_b, a_w_ga, a_b_ga, a_w_gx, a_b_gx,
           a_lambda, a_w_out, kv_norm, w_kv, b_f, b_pre_norm, b_post_norm, b_w_in, b_w_out):
    n_a = a_w_in.shape[0]
    n_b = b_w_in.shape[0]
    d = x_prompt.shape[-1]
    n_heads = b_f.shape[0]
    hd = d // n_heads
    assert 2 * hd == LANES and n_heads <= LANES

    a_win = a_w_in.astype(BF16)
    a_wout = a_w_out.astype(BF16)
    a_wg = jnp.concatenate([a_w_ga, a_w_gx], axis=-1).astype(BF16)
    w_kv_t = w_kv.T.astype(BF16)
    wkt, wvt, wft = w_kv_t[:d], w_kv_t[d:2 * d], w_kv_t[2 * d:]
    wk = w_kv[:, :d].astype(BF16)
    wv = w_kv[:, d:2 * d].astype(BF16)
    wf = jnp.pad(w_kv[:, 2 * d:], ((0, 0), (0, LANES - n_heads))).astype(BF16)
    bfr = jnp.pad(b_f, (0, LANES - n_heads)).reshape(1, LANES)
    bfc = b_f.reshape(n_heads, 1)
    b_win = b_w_in.astype(BF16)
    b_wout = b_w_out.astype(BF16)
    b_win_t = jnp.swapaxes(b_w_in, 1, 2).astype(BF16)

    def a_layer(l, x, h0, c0, tt, nb):
        return _layer_a(x, h0, c0, a_pre_norm[l], a_post_norm[l], a_win[l], a_conv_w[l], a_conv_b[l],
                        a_wg[l], a_b_ga[l], a_b_gx[l], a_lambda[l], a_wout[l], tt=tt, nb=nb)

    bp, tp, _ = x_prompt.shape
    x = x_prompt
    rnn_p, conv_p = [], []
    for l in range(n_a):
        x, hl, cn = a_layer(l, x, jnp.zeros((bp, 1, d), F32), jnp.zeros((bp, CONV_W - 1, d), F32),
                            tt=_tile(tp, RGLRU_TILE), nb=1)
        rnn_p.append(hl.reshape(bp, d))
        conv_p.append(cn)
    tt = _tile(tp, PROMPT_TILE)
    kt_p, vt_p, lft_p, ct_p, kaug_p, vtb_p = _kv_prompt(x, kv_norm, wk, wkt, wvt, wf, wft, bfr, bfc,
                                                        tt=tt, n_heads=n_heads)
    for l in range(n_b):
        qt, gt = _qg_prompt(x, b_pre_norm[l], b_win_t[l], ct_p, tt=tt, n_heads=n_heads)
        ot = _prompt_attention(qt, kaug_p, vtb_p, blk=tt, hd=hd)
        x = _out_prompt(ot, gt, x, b_wout[l], b_post_norm[l], tt=tt)
    y_prompt = x
    k_p = jnp.transpose(kt_p.reshape(bp, n_heads, hd, tp), (0, 3, 1, 2))
    v_p = jnp.transpose(vt_p.reshape(bp, n_heads, hd, tp), (0, 3, 1, 2))
    lf_p = jnp.swapaxes(lft_p, 1, 2)

    bs, ts, _ = x_sample.shape
    rows_s = bs * ts
    n_pool, page = cache_k.shape[:2]
    x = jnp.swapaxes(x_sample, 0, 1).reshape(1, rows_s, d)
    rnn_s, conv_s = [], []
    for l in range(n_a):
        c0 = jnp.swapaxes(state_conv[l], 0, 1).reshape(1, (CONV_W - 1) * bs, d)
        x, hl, cn = a_layer(l, x, state_rnn[l].reshape(1, bs, d), c0, tt=ts, nb=bs)
        rnn_s.append(hl.reshape(bs, d))
        conv_s.append(jnp.swapaxes(cn.reshape(CONV_W - 1, bs, d), 0, 1))
    x = jnp.swapaxes(x.reshape(ts, bs, d), 0, 1).reshape(rows_s, d)
    k_s, v_s, lf_s = _kv_sample(x, kv_norm, wk, wv, wf, bfr, n_heads=n_heads)
    lft_new = jnp.pad(jnp.swapaxes(lf_s.reshape(bs, ts, n_heads), 1, 2),
                      ((0, 0), (0, 0), (0, page - ts)))
    cache_kt = jnp.transpose(cache_k, (0, 2, 3, 1)).reshape(n_pool, d, page)
    cache_vt = jnp.transpose(cache_v, (0, 2, 3, 1)).reshape(n_pool, d, page)
    cache_lft = jnp.swapaxes(cache_logf, 1, 2)
    for l in range(n_b):
        q, gate = _qg_sample(x, b_pre_norm[l], b_win[l], n_heads=n_heads)
        o = _decode_attention(page_table, q.reshape(bs, ts, d), k_s.reshape(bs, ts, d),
                              v_s.reshape(bs, ts, d), lft_new, cache_kt, cache_vt, cache_lft,
                              n_heads=n_heads)
        x = _out_sample(o.reshape(rows_s, d), gate, x, b_wout[l], b_post_norm[l])
    y_sample = x.reshape(bs, ts, d)

    return (y_prompt, y_sample, k_p, v_p, lf_p,
            jnp.stack(rnn_p), jnp.stack(conv_p),
            k_s.reshape(bs, ts, n_heads, hd), v_s.reshape(bs, ts, n_heads, hd),
            lf_s.reshape(bs, ts, n_heads),
            jnp.stack(rnn_s), jnp.stack(conv_s))
```

```python
import functools

import jax
import jax.numpy as jnp
from jax import lax
from jax.experimental import pallas as pl
from jax.experimental.pallas import tpu as pltpu

F32 = jnp.float32
BF16 = jnp.bfloat16

EPS = 1e-6
RGLRU_C = 8.0
CONV_W = 4
LANES = 128
SUBLANES = 8
BF16_ROWS = 16
N_AUG = 3
LOG2E = 1.4426950408889634
VMEM_LIMIT = 56 * 1024 * 1024
RGLRU_TILE = 256
PROMPT_TILE = 512


def _tile(n, want):
    t = min(n, want)
    while n % t or t % LANES:
        t -= LANES
    return t

_NT = (((1,), (1,)), ((), ()))
_TN = (((0,), (0,)), ((), ()))


def _rms(x, g):
    ms = jnp.mean(x * x, axis=-1, keepdims=True)
    return x * lax.rsqrt(ms + EPS) * g


def _softplus(x):
    return jnp.maximum(x, 0.0) + jnp.log1p(jnp.exp(-jnp.abs(x)))


def _log_sigmoid(x):
    return jnp.minimum(x, 0.0) - jnp.log1p(jnp.exp(-jnp.abs(x)))


def _expm1_of_square(x, u):
    d = u - 1.0
    edge = (d == 0.0) | (d == -1.0)
    r = d * x / jnp.log(jnp.where(edge, 0.5, u))
    return jnp.where(d == 0.0, x, jnp.where(d == -1.0, -1.0, r))


def _silu(x):
    return x * jax.nn.sigmoid(x)


def _split3(x):
    hi = x.astype(BF16).astype(F32)
    r1 = x - hi
    mid = r1.astype(BF16).astype(F32)
    lo = (r1 - mid).astype(BF16).astype(F32)
    return hi, mid, lo


def _exact_dot(ones_lhs, x):
    hi, mid, lo = _split3(x)
    d = lambda p: jnp.dot(ones_lhs, p.astype(BF16), preferred_element_type=F32)
    return d(hi) + d(mid) + d(lo)


def _exact_dot_rhs(x, ones_rhs):
    hi, mid, lo = _split3(x)
    d = lambda p: jnp.dot(p.astype(BF16), ones_rhs, preferred_element_type=F32)
    return d(hi) + d(mid) + d(lo)


def _const_spec(shape):
    nd = len(shape)
    return pl.BlockSpec(shape, lambda *_: (0,) * nd)


def _params(*sem):
    return pltpu.CompilerParams(dimension_semantics=sem, vmem_limit_bytes=VMEM_LIMIT)


SEQS_PER_STEP = 4


def _rglru_gates(uc, wg_ref, bga_ref, bgx_ref, lam_ref, a_s, b_s):
    d = uc.shape[-1]
    nblk = wg_ref.shape[0]
    blk = d // nblk
    rate = (-RGLRU_C) * _softplus(-lam_ref[...])
    ucb = uc.astype(BF16)
    for n in range(nblk):
        sl = slice(n * blk, (n + 1) * blk)
        gz = jnp.dot(ucb[:, sl], wg_ref[n], preferred_element_type=F32)
        r = jax.nn.sigmoid(gz[:, :blk] + bga_ref[:, sl])
        i = jax.nn.sigmoid(gz[:, blk:] + bgx_ref[:, sl])
        log_a = r * rate[:, sl]
        a = jnp.exp(log_a)
        a_s[:, sl] = a
        b_s[:, sl] = jnp.sqrt(-_expm1_of_square(2.0 * log_a, a * a)) * (i * uc[:, sl])


def _layer_a_seq_kernel(x_ref, h0_ref, c0_ref, pre_ref, post_ref, win_ref, cw_ref, cb_ref,
                        wg_ref, bga_ref, bgx_ref, lam_ref, wout_ref,
                        y_ref, hl_ref, cn_ref,
                        ubuf, a_s, b_s, g_s, hcar, tail_s, perm_s, *, tt):
    t = pl.program_id(1)
    n_seq, _, d = x_ref.shape
    seg = tt // SUBLANES
    n_tail = CONV_W - 1
    head = n_tail * SUBLANES

    @pl.when(t == 0)
    def _():
        hcar[...] = h0_ref[...]
        tail_s[:, 0:n_tail, :] = c0_ref[...]
        r = lax.broadcasted_iota(jnp.int32, (tt, tt), 0)
        c = lax.broadcasted_iota(jnp.int32, (tt, tt), 1)
        time_of = lambda i: (i & (SUBLANES - 1)) * seg + (i >> 3)
        perm_s[0] = jnp.where(c == time_of(r), 1.0, 0.0).astype(BF16)
        perm_s[1] = jnp.where(r == time_of(c), 1.0, 0.0).astype(BF16)

    for b in range(n_seq):
        xn = _rms(x_ref[b], pre_ref[...]).astype(BF16)
        xp = jnp.dot(perm_s[0], xn, preferred_element_type=F32).astype(BF16)
        z = jnp.dot(xp, win_ref[...], preferred_element_type=F32)
        u = z[:, :d]
        g_s[b] = z[:, d:]

        ubuf[b, head:head + tt, :] = u
        sub = lax.broadcasted_iota(jnp.int32, (SUBLANES, d), 0)
        for j in range(n_tail):
            grp = u[(seg - n_tail + j) * SUBLANES:(seg - n_tail + j + 1) * SUBLANES, :]
            ubuf[b, j * SUBLANES:(j + 1) * SUBLANES, :] = jnp.where(
                sub == 0, tail_s[b, j:j + 1, :], pltpu.roll(grp, 1, 0))
            tail_s[b, j:j + 1, :] = grp[SUBLANES - 1:SUBLANES, :]
        cn_ref[b] = tail_s[b, 0:n_tail, :]

        cw = cw_ref[...]
        uc = u * cw[n_tail:CONV_W, :] + cb_ref[...]
        for j in range(n_tail):
            uc = uc + ubuf[b, j * SUBLANES:j * SUBLANES + tt, :] * cw[j:j + 1, :]

        _rglru_gates(uc, wg_ref, bga_ref, bgx_ref, lam_ref, a_s.at[b], b_s.at[b])

    def body(k, carry):
        rows = pl.ds(pl.multiple_of(k * SUBLANES, SUBLANES), SUBLANES)
        out = []
        for b in range(n_seq):
            h, prod = carry[b]
            a = a_s[b, rows, :]
            h = a * h + b_s[b, rows, :]
            prod = prod * a
            b_s[b, rows, :] = h
            a_s[b, rows, :] = prod
            out.append((h, prod))
        return tuple(out)
    init = (jnp.zeros((SUBLANES, d), F32), jnp.ones((SUBLANES, d), F32))
    ends = lax.fori_loop(0, seg, body, (init,) * n_seq)

    for b in range(n_seq):
        h_end, a_end = ends[b]
        c = hcar[b]
        enter = []
        for s in range(SUBLANES):
            enter.append(c)
            c = h_end[s:s + 1, :] + a_end[s:s + 1, :] * c
        hcar[b] = c
        hl_ref[b] = c
        enter = jnp.concatenate(enter, axis=0)[None]

        h = (b_s[b].reshape(seg, SUBLANES, d) + a_s[b].reshape(seg, SUBLANES, d) * enter)
        hg = (h.reshape(tt, d) * _silu(g_s[b])).astype(BF16)
        hg = jnp.dot(perm_s[1], hg, preferred_element_type=F32).astype(BF16)
        y = jnp.dot(hg, wout_ref[...], preferred_element_type=F32)
        y_ref[b] = x_ref[b] + _rms(y, post_ref[...])


def _layer_a_kernel(x_ref, h0_ref, c0_ref, pre_ref, post_ref, win_ref, cw_ref, cb_ref,
                    wg_ref, bga_ref, bgx_ref, lam_ref, wout_ref,
                    y_ref, hl_ref, cn_ref,
                    ubuf, a_s, b_s, hcar, *, tt, nb, off):
    t = pl.program_id(1)
    rows = tt * nb
    d = x_ref.shape[-1]
    tail = (CONV_W - 1) * nb

    @pl.when(t == 0)
    def _():
        hcar[...] = h0_ref[0]
        ubuf[off - tail:off, :] = c0_ref[0]

    x = x_ref[0]
    xn = _rms(x, pre_ref[...])
    z = jnp.dot(xn.astype(BF16), win_ref[...], preferred_element_type=F32)
    u = z[:, :d]
    g = z[:, d:]

    ubuf[off:off + rows, :] = u
    cw = cw_ref[...]
    uc = u * cw[CONV_W - 1:CONV_W, :] + cb_ref[...]
    for j in range(CONV_W - 1):
        lo = off - (CONV_W - 1 - j) * nb
        uc = uc + ubuf[lo:lo + rows, :] * cw[j:j + 1, :]
    ubuf[off - tail:off, :] = ubuf[off + rows - tail:off + rows, :]

    _rglru_gates(uc, wg_ref, bga_ref, bgx_ref, lam_ref, a_s, b_s)

    h = hcar[...]
    for s in range(tt):
        sl = slice(s * nb, (s + 1) * nb)
        h = a_s[sl, :] * h + b_s[sl, :]
        b_s[sl, :] = h
    hcar[...] = h
    hl_ref[0] = h
    cn_ref[0] = ubuf[off - tail:off, :]

    hg = b_s[...] * _silu(g)
    y = jnp.dot(hg.astype(BF16), wout_ref[...], preferred_element_type=F32)
    y_ref[0] = x + _rms(y, post_ref[...])


def _layer_a(x, h0, c0, pre, post, win, cw, cb, wg, bga, bgx, lam, wout, *, tt, nb):
    g_, rows_total, d = x.shape
    rows = tt * nb
    nt = rows_total // rows
    assert nt * rows == rows_total and tt >= CONV_W - 1
    assert (nb == 1 and tt % SUBLANES == 0) or nb % SUBLANES == 0
    tail = (CONV_W - 1) * nb
    off = -(-tail // SUBLANES) * SUBLANES
    row = lambda v: v.reshape(1, d)
    if nb == 1:
        assert tt % (SUBLANES * SUBLANES) == 0 and tt // SUBLANES > CONV_W - 1
        sps = SEQS_PER_STEP if g_ % SEQS_PER_STEP == 0 else 1
        kern = functools.partial(_layer_a_seq_kernel, tt=tt)
        scratch = [
            pltpu.VMEM((sps, tail * SUBLANES + rows, d), F32),
            pltpu.VMEM((sps, rows, d), F32),
            pltpu.VMEM((sps, rows, d), F32),
            pltpu.VMEM((sps, rows, d), F32),
            pltpu.VMEM((sps, 1, d), F32),
            pltpu.VMEM((sps, SUBLANES, d), F32),
            pltpu.VMEM((2, tt, tt), BF16),
        ]
    else:
        sps = 1
        kern = functools.partial(_layer_a_kernel, tt=tt, nb=nb, off=off)
        scratch = [
            pltpu.VMEM((off + rows, d), F32),
            pltpu.VMEM((rows, d), F32),
            pltpu.VMEM((rows, d), F32),
            pltpu.VMEM((nb, d), F32),
        ]
    return pl.pallas_call(
        kern,
        grid=(g_ // sps, nt),
        in_specs=[
            pl.BlockSpec((sps, rows, d), lambda g, t: (g, t, 0)),
            pl.BlockSpec((sps, nb, d), lambda g, t: (g, 0, 0)),
            pl.BlockSpec((sps, tail, d), lambda g, t: (g, 0, 0)),
            _const_spec((1, d)), _const_spec((1, d)),
            _const_spec(win.shape), _const_spec(cw.shape), _const_spec((1, d)),
            _const_spec(wg.shape), _const_spec((1, d)), _const_spec((1, d)), _const_spec((1, d)),
            _const_spec(wout.shape),
        ],
        out_specs=[
            pl.BlockSpec((sps, rows, d), lambda g, t: (g, t, 0)),
            pl.BlockSpec((sps, nb, d), lambda g, t: (g, 0, 0)),
            pl.BlockSpec((sps, tail, d), lambda g, t: (g, 0, 0)),
        ],
        out_shape=[
            jax.ShapeDtypeStruct((g_, rows_total, d), F32),
            jax.ShapeDtypeStruct((g_, nb, d), F32),
            jax.ShapeDtypeStruct((g_, tail, d), F32),
        ],
        scratch_shapes=scratch,
        compiler_params=_params("parallel", "arbitrary"),
        name="rglru_layer",
    )(x, h0, c0, row(pre), row(post), win, cw, row(cb), wg, row(bga), row(bgx), row(lam), wout)


def _aug_base(h, hd):
    upper = ((h * hd) % LANES) != 0
    return (0 if upper else hd), upper


def _key_block(k, col, h, hd):
    rows = k.shape[0]
    pair = (h * hd) // LANES
    blk = k[:, pair * LANES:(pair + 1) * LANES]
    a0, upper = _aug_base(h, hd)
    lane = lax.broadcasted_iota(jnp.int32, (rows, LANES), 1)
    data = (lane >= hd) if upper else (lane < hd)
    hi, mid, lo = _split3(-LOG2E * col)
    v0 = a0 + N_AUG
    aug = jnp.where(lane == v0, hi,
                    jnp.where(lane == v0 + 1, mid,
                              jnp.where(lane == v0 + 2, lo,
                                        jnp.where((lane >= a0) & (lane < v0), 1.0, 0.0))))
    return jnp.where(data, blk, aug).astype(BF16)


def _query_block_t(qt, row, h, hd):
    cols = qt.shape[1]
    a0, upper = _aug_base(h, hd)
    sub = lax.broadcasted_iota(jnp.int32, (SUBLANES, cols), 0)
    hi, mid, lo = _split3(LOG2E * row)
    aug = jnp.where(sub == 0, hi,
                    jnp.where(sub == 1, mid,
                              jnp.where(sub == 2, lo,
                                        jnp.where(sub < 2 * N_AUG, 1.0, 0.0))))
    pad = jnp.zeros((LANES - hd - SUBLANES, cols), F32)
    data = qt[h * hd:(h + 1) * hd, :]
    parts = [aug, pad, data] if upper else [data, aug, pad]
    return jnp.concatenate(parts, axis=0).astype(BF16)


def _kv_prompt_kernel(x_ref, g_ref, wk_ref, wkvt_ref, wf_ref, bfr_ref, bfc_ref,
                      kt_ref, vt_ref, lft_ref, ct_ref, kp_ref, vtb_ref, carry_r, carry_c,
                      *, n_heads):
    x = x_ref[0]
    tt, d = x.shape
    hd = d // n_heads
    xn = _rms(x, g_ref[...]).astype(BF16)
    nt_dot = lambda lo, hi: lax.dot_general(wkvt_ref[lo:hi, :], xn, _NT,
                                            preferred_element_type=F32)
    kt = nt_dot(0, d)
    vt = nt_dot(d, 2 * d)
    k = jnp.dot(xn, wk_ref[...], preferred_element_type=F32)
    lf = _log_sigmoid(jnp.dot(xn, wf_ref[...], preferred_element_type=F32) + bfr_ref[...])
    lft = _log_sigmoid(nt_dot(2 * d, 2 * d + n_heads) + bfc_ref[...])
    kt_ref[0] = kt
    vt_ref[0] = vt
    vtb_ref[0] = vt.astype(BF16)
    lft_ref[0] = lft

    @pl.when(pl.program_id(1) == 0)
    def _():
        carry_r[...] = jnp.zeros_like(carry_r)
        carry_c[...] = jnp.zeros_like(carry_c)

    r = lax.broadcasted_iota(jnp.int32, (tt, tt), 0)
    c = lax.broadcasted_iota(jnp.int32, (tt, tt), 1)
    cs = _exact_dot(jnp.where(r >= c, 1.0, 0.0).astype(BF16), lf) + carry_r[...]
    cst = _exact_dot_rhs(lft, jnp.where(r <= c, 1.0, 0.0).astype(BF16)) + carry_c[...]
    carry_r[...] = cs[tt - 1:tt, :]
    carry_c[...] = cst[:, tt - 1:tt]
    ct_ref[0] = cst
    for h in range(n_heads):
        kp_ref[0, h] = _key_block(k, cs[:, h:h + 1], h, hd)


def _kv_prompt(x, g, wk, wkvt, wf, bfr, bfc, *, tt, n_heads):
    b, t, d = x.shape
    nt = t // tt
    assert nt * tt == t
    feat = lambda n, dt: (pl.BlockSpec((1, n, tt), lambda i, j: (i, 0, j)),
                          jax.ShapeDtypeStruct((b, n, t), dt))
    outs = [feat(d, F32), feat(d, F32), feat(n_heads, F32), feat(n_heads, F32),
            (pl.BlockSpec((1, n_heads, tt, LANES), lambda i, j: (i, 0, j, 0)),
             jax.ShapeDtypeStruct((b, n_heads, t, LANES), BF16)),
            feat(d, BF16)]
    return pl.pallas_call(
        functools.partial(_kv_prompt_kernel, n_heads=n_heads),
        grid=(b, nt),
        in_specs=[pl.BlockSpec((1, tt, d), lambda i, j: (i, j, 0)), _const_spec((1, d)),
                  _const_spec(wk.shape), _const_spec(wkvt.shape), _const_spec(wf.shape),
                  _const_spec(bfr.shape), _const_spec(bfc.shape)],
        out_specs=[o[0] for o in outs],
        out_shape=[o[1] for o in outs],
        scratch_shapes=[pltpu.VMEM((1, LANES), F32), pltpu.VMEM((n_heads, 1), F32)],
        compiler_params=_params("parallel", "arbitrary"),
        name="kv_prompt",
    )(x, g.reshape(1, d), wk, wkvt, wf, bfr, bfc)


def _qg_prompt_kernel(x_ref, g_ref, wt_ref, ct_ref, qt_ref, gt_ref, *, n_heads):
    x = x_ref[0]
    d = x.shape[-1]
    hd = d // n_heads
    xn = _rms(x, g_ref[...]).astype(BF16)
    zt = lax.dot_general(wt_ref[...], xn, _NT, preferred_element_type=F32)
    gt_ref[0] = zt[d:, :].astype(BF16)
    qt = zt[:d, :] * (hd ** -0.5 * LOG2E)
    ct = ct_ref[0]
    for h in range(n_heads):
        qt_ref[0, h] = _query_block_t(qt, ct[h:h + 1, :], h, hd)


def _qg_prompt(x, g, wt, ct, *, tt, n_heads):
    b, t, d = x.shape
    nt = t // tt
    assert nt * tt == t
    return pl.pallas_call(
        functools.partial(_qg_prompt_kernel, n_heads=n_heads),
        grid=(b, nt),
        in_specs=[pl.BlockSpec((1, tt, d), lambda i, j: (i, j, 0)), _const_spec((1, d)),
                  _const_spec(wt.shape),
                  pl.BlockSpec((1, n_heads, tt), lambda i, j: (i, 0, j))],
        out_specs=[pl.BlockSpec((1, n_heads, LANES, tt), lambda i, j: (i, 0, 0, j)),
                   pl.BlockSpec((1, d, tt), lambda i, j: (i, 0, j))],
        out_shape=[jax.ShapeDtypeStruct((b, n_heads, LANES, t), BF16),
                   jax.ShapeDtypeStruct((b, d, t), BF16)],
        compiler_params=_params("parallel", "parallel"),
        name="qg_prompt",
    )(x, g.reshape(1, d), wt, ct)


def _attn_kernel(qi_tab, ki_tab, q_ref, k_ref, v_ref, o_ref, m_s, l_s, al_s, mc_s, acc_s, *bufs, hd):
    st_bufs, pt_bufs = bufs[:4], bufs[4:]
    step = pl.program_id(1)
    qi = qi_tab[step]
    ki = ki_tab[step]
    n_heads, _, tq = q_ref.shape[1:]
    tk = k_ref.shape[2]
    n_pairs = n_heads // 2
    assert n_heads % 4 == 0 and n_pairs >= 4

    @pl.when(ki == 0)
    def _():
        m_s[...] = jnp.full(m_s.shape, -jnp.inf, F32)
        l_s[...] = jnp.zeros_like(l_s)
        acc_s[...] = jnp.zeros_like(acc_s)

    def run(diag):
        def scores(pair, s):
            for j in range(2):
                h = 2 * pair + j
                st = jnp.dot(k_ref[0, h], q_ref[0, h], preferred_element_type=F32)
                if diag:
                    keep = (lax.broadcasted_iota(jnp.int32, (tk, tq), 0)
                            <= lax.broadcasted_iota(jnp.int32, (tk, tq), 1))
                    st = jnp.where(keep, st, -jnp.inf)
                st_bufs[2 * s + j][...] = st
                mc_s[2 * s + j] = jnp.max(st, axis=0, keepdims=True)

        def softmax(pair, s):
            for j in range(2):
                h = 2 * pair + j
                m_prev = m_s[h]
                m_new = jnp.maximum(m_prev, mc_s[2 * s + j])
                m_s[h] = m_new
                al_s[h] = jnp.exp2(m_prev - m_new)
                pt_bufs[2 * s + j][...] = jnp.exp2(st_bufs[2 * s + j][...] - m_new).astype(BF16)

        ones_rows = jnp.ones((BF16_ROWS, tk), BF16)

        def values(pair, s):
            for j in range(2):
                h = 2 * pair + j
                rows = pl.ds(h * hd if isinstance(h, int) else pl.multiple_of(h * hd, hd), hd)
                pv = jnp.dot(jnp.concatenate([v_ref[0, rows, :], ones_rows], axis=0),
                             pt_bufs[2 * s + j][...], preferred_element_type=F32)
                alpha = al_s[h]
                acc_s[rows, :] = acc_s[rows, :] * alpha + pv[:hd, :]
                l_s[h] = alpha * l_s[h] + pv[hd:hd + 1, :]

        for t in range(n_pairs + 2):
            if t < n_pairs:
                scores(t, t % 2)
            if 1 <= t <= n_pairs:
                softmax(t - 1, (t - 1) % 2)
            if t >= 2:
                values(t - 2, t % 2)

    @pl.when(ki < qi)
    def _():
        run(False)

    @pl.when(ki == qi)
    def _():
        run(True)
        for h in range(n_heads):
            rows = slice(h * hd, (h + 1) * hd)
            o_ref[0, rows, :] = (acc_s[rows, :] * (1.0 / l_s[h])).astype(o_ref.dtype)


def _prompt_attention(qt, kp, vtb, *, blk, hd):
    b, n_heads, _, t = qt.shape
    d = vtb.shape[1]
    nq = t // blk
    assert nq * blk == t
    qi_l, ki_l = [], []
    for i in range(nq):
        for j in range(i + 1):
            qi_l.append(i)
            ki_l.append(j)
    grid_spec = pltpu.PrefetchScalarGridSpec(
        num_scalar_prefetch=2,
        grid=(b, len(qi_l)),
        in_specs=[
            pl.BlockSpec((1, n_heads, LANES, blk), lambda i, s, qt_, kt_: (i, 0, 0, qt_[s])),
            pl.BlockSpec((1, n_heads, blk, LANES), lambda i, s, qt_, kt_: (i, 0, kt_[s], 0)),
            pl.BlockSpec((1, d, blk), lambda i, s, qt_, kt_: (i, 0, kt_[s])),
        ],
        out_specs=pl.BlockSpec((1, d, blk), lambda i, s, qt_, kt_: (i, 0, qt_[s])),
        scratch_shapes=[
            pltpu.VMEM((n_heads, 1, blk), F32),
            pltpu.VMEM((n_heads, 1, blk), F32),
            pltpu.VMEM((n_heads, 1, blk), F32),
            pltpu.VMEM((4, 1, blk), F32),
            pltpu.VMEM((d, blk), F32),
        ] + [pltpu.VMEM((blk, blk), F32)] * 4 + [pltpu.VMEM((blk, blk), BF16)] * 4,
    )
    return pl.pallas_call(
        functools.partial(_attn_kernel, hd=hd),
        grid_spec=grid_spec,
        out_shape=jax.ShapeDtypeStruct((b, d, t), BF16),
        compiler_params=_params("parallel", "arbitrary"),
        name="prompt_attention",
    )(jnp.asarray(qi_l, jnp.int32), jnp.asarray(ki_l, jnp.int32), qt, kp, vtb)


def _out_prompt_kernel(ot_ref, gt_ref, x_ref, w_ref, post_ref, y_ref):
    hgt = ot_ref[0].astype(F32) * _silu(gt_ref[0].astype(F32))
    y = lax.dot_general(hgt.astype(BF16), w_ref[...], _TN, preferred_element_type=F32)
    y_ref[0] = x_ref[0] + _rms(y, post_ref[...])


def _out_prompt(ot, gt, x, wt, post, *, tt):
    b, t, d = x.shape
    nt = t // tt
    assert nt * tt == t
    row_spec = pl.BlockSpec((1, tt, d), lambda i, j: (i, j, 0))
    feat_spec = pl.BlockSpec((1, d, tt), lambda i, j: (i, 0, j))
    return pl.pallas_call(
        _out_prompt_kernel,
        grid=(b, nt),
        in_specs=[feat_spec, feat_spec, row_spec, _const_spec(wt.shape), _const_spec((1, d))],
        out_specs=row_spec,
        out_shape=jax.ShapeDtypeStruct((b, t, d), F32),
        compiler_params=_params("parallel", "parallel"),
        name="out_prompt",
    )(ot, gt, x, wt, post.reshape(1, d))


def _kv_sample_kernel(x_ref, g_ref, wk_ref, wv_ref, wf_ref, bfr_ref, k_ref, v_ref, lf_ref,
                      *, n_heads):
    xn = _rms(x_ref[...], g_ref[...]).astype(BF16)
    k_ref[...] = jnp.dot(xn, wk_ref[...], preferred_element_type=F32)
    v_ref[...] = jnp.dot(xn, wv_ref[...], preferred_element_type=F32)
    lf = _log_sigmoid(jnp.dot(xn, wf_ref[...], preferred_element_type=F32) + bfr_ref[...])
    lf_ref[...] = lf[:, :n_heads]


def _kv_sample(x, g, wk, wv, wf, bfr, *, n_heads):
    rows, d = x.shape
    return pl.pallas_call(
        functools.partial(_kv_sample_kernel, n_heads=n_heads),
        out_shape=[jax.ShapeDtypeStruct((rows, d), F32), jax.ShapeDtypeStruct((rows, d), F32),
                   jax.ShapeDtypeStruct((rows, n_heads), F32)],
        compiler_params=pltpu.CompilerParams(vmem_limit_bytes=VMEM_LIMIT),
        name="kv_sample",
    )(x, g.reshape(1, d), wk, wv, wf, bfr)


def _qg_sample_kernel(x_ref, g_ref, w_ref, q_ref, gate_ref, *, n_heads):
    d = x_ref.shape[-1]
    xn = _rms(x_ref[...], g_ref[...]).astype(BF16)
    z = jnp.dot(xn, w_ref[...], preferred_element_type=F32)
    q_ref[...] = z[:, :d] * ((d // n_heads) ** -0.5)
    gate_ref[...] = z[:, d:]


def _qg_sample(x, g, w, *, n_heads):
    rows, d = x.shape
    return pl.pallas_call(
        functools.partial(_qg_sample_kernel, n_heads=n_heads),
        out_shape=[jax.ShapeDtypeStruct((rows, d), F32), jax.ShapeDtypeStruct((rows, d), F32)],
        compiler_params=pltpu.CompilerParams(vmem_limit_bytes=VMEM_LIMIT),
        name="qg_sample",
    )(x, g.reshape(1, d), w)


def _out_sample_kernel(o_ref, gate_ref, x_ref, w_ref, post_ref, y_ref):
    hg = o_ref[...] * _silu(gate_ref[...])
    y = jnp.dot(hg.astype(BF16), w_ref[...], preferred_element_type=F32)
    y_ref[...] = x_ref[...] + _rms(y, post_ref[...])


def _out_sample(o, gate, x, w, post):
    rows, d = x.shape
    return pl.pallas_call(
        _out_sample_kernel,
        out_shape=jax.ShapeDtypeStruct((rows, d), F32),
        compiler_params=pltpu.CompilerParams(vmem_limit_bytes=VMEM_LIMIT),
        name="out_sample",
    )(o, gate, x, w, post.reshape(1, d))


PAGE_GROUP = 4
RING_GROUPS = 5
N_SLOTS = RING_GROUPS * PAGE_GROUP


def _decode_kernel(pt_ref, q_ref, kn_ref, vn_ref, lfn_ref, kc_hbm, vc_hbm, lfc_hbm, o_ref,
                   kvbuf, lbuf, sem_kv, sem_l, s_ref, p_ref, acc_ref, qbd_ref, *, n_heads):
    seq = pl.program_id(0)
    n_seq = pl.num_programs(0)
    n_pages = pt_ref.shape[1]
    d, page = kvbuf.shape[1:]
    nq = q_ref.shape[1]
    hd = d // n_heads
    rows = n_heads * nq
    assert rows == LANES and page == LANES and n_pages % PAGE_GROUP == 0
    k_groups = n_pages // PAGE_GROUP
    n_groups = 2 * k_groups

    def kv_copy(src_hbm, page_id, slot):
        return pltpu.make_async_copy(src_hbm.at[page_id], kvbuf.at[slot], sem_kv.at[slot])

    def lf_copy(page_id, slot):
        return pltpu.make_async_copy(lfc_hbm.at[page_id], lbuf.at[slot], sem_l.at[slot])

    first = seq * n_groups

    def group_base(gid):
        return lax.rem(gid, RING_GROUPS) * PAGE_GROUP

    def start_group(gid):
        base = group_base(gid)
        sq = lax.div(gid, n_groups)
        grp = lax.rem(gid, n_groups)
        live = gid < n_seq * n_groups

        @pl.when(live & (grp < k_groups))
        def _():
            for j in range(PAGE_GROUP):
                pid = pt_ref[sq, grp * PAGE_GROUP + j]
                kv_copy(kc_hbm, pid, base + j).start()
                lf_copy(pid, base + j).start()

        @pl.when(live & (grp >= k_groups))
        def _():
            for j in range(PAGE_GROUP):
                pid = pt_ref[sq, (grp - k_groups) * PAGE_GROUP + j]
                kv_copy(vc_hbm, pid, base + j).start()

    @pl.when(seq == 0)
    def _():
        for g in range(RING_GROUPS):
            start_group(jnp.int32(g))

    q = q_ref[0]
    lane_head = lax.broadcasted_iota(jnp.int32, (nq, d), 1) // hd
    qbd_ref[...] = jnp.concatenate([jnp.where(lane_head == h, q, 0.0) for h in range(n_heads)],
                                   axis=0).astype(BF16)

    r = lax.broadcasted_iota(jnp.int32, (page, 2 * page), 0)
    c = lax.broadcasted_iota(jnp.int32, (page, 2 * page), 1)
    tri_ones = jnp.where((r <= c) | (c >= page), 1.0, 0.0).astype(BF16)

    def head_rows(ct):
        return jnp.concatenate(
            [jnp.broadcast_to(ct[h:h + 1, :], (nq, ct.shape[1])) for h in range(n_heads)], axis=0)

    def k_group(g, carry):
        gid = first + g
        base = group_base(gid)
        for j in range(PAGE_GROUP):
            kv_copy(kc_hbm, 0, base + j).wait()
            lf_copy(0, base + j).wait()
        for j in range(PAGE_GROUP):
            sc = jnp.dot(qbd_ref[...], kvbuf[base + j].astype(BF16), preferred_element_type=F32)
            cum = _exact_dot_rhs(lbuf[base + j], tri_ones)
            s_ref[g * PAGE_GROUP + j] = sc - head_rows(carry + cum[:, :page])
            carry = carry + cum[:, page:]
        start_group(gid + RING_GROUPS)
        return carry

    past_total = lax.fori_loop(0, k_groups, k_group, jnp.zeros((n_heads, page), F32))

    ct_new = past_total + _exact_dot_rhs(lfn_ref[0], tri_ones)[:, :page]
    kn = jnp.concatenate([kn_ref[0], jnp.zeros((page - nq, d), F32)], axis=0).astype(BF16)
    sc_new = lax.dot_general(qbd_ref[...], kn, _NT, preferred_element_type=F32)
    key_idx = lax.broadcasted_iota(jnp.int32, (rows, page), 1)
    q_idx = lax.rem(lax.broadcasted_iota(jnp.int32, (rows, page), 0), nq)
    ct_rows = head_rows(ct_new)
    cq = jnp.sum(jnp.where(key_idx == q_idx, ct_rows, 0.0), axis=1, keepdims=True)
    s_ref[n_pages] = jnp.where(key_idx <= q_idx, sc_new - ct_rows, -jnp.inf)

    def max_body(i, m):
        return jnp.maximum(m, s_ref[i])
    m_el = lax.fori_loop(0, n_pages + 1, max_body, jnp.full((rows, page), -jnp.inf, F32))
    m = jnp.max(m_el, axis=1, keepdims=True) + cq

    def p_body(i, l):
        pe = jnp.exp((s_ref[i] + cq) - m)
        p_ref[i] = pe.astype(BF16)
        return l + pe
    l_el = lax.fori_loop(0, n_pages + 1, p_body, jnp.zeros((rows, page), F32))
    l = jnp.sum(l_el, axis=1, keepdims=True)

    vn = jnp.concatenate([vn_ref[0], jnp.zeros((page - nq, d), F32)], axis=0).astype(BF16)
    acc_ref[...] = jnp.dot(p_ref[n_pages], vn, preferred_element_type=F32)

    def v_group(g, carry):
        gid = first + k_groups + g
        base = group_base(gid)
        for j in range(PAGE_GROUP):
            kv_copy(vc_hbm, 0, base + j).wait()
        upd = None
        for j in range(PAGE_GROUP):
            pv = lax.dot_general(p_ref[g * PAGE_GROUP + j], kvbuf[base + j].astype(BF16), _NT,
                                 preferred_element_type=F32)
            upd = pv if upd is None else upd + pv
        acc_ref[...] += upd
        start_group(gid + RING_GROUPS)
        return carry

    lax.fori_loop(0, k_groups, v_group, 0)

    acc = acc_ref[...] * (1.0 / l)
    out = jnp.zeros((nq, d), F32)
    for h in range(n_heads):
        out = out + jnp.where(lane_head == h, acc[h * nq:(h + 1) * nq, :], 0.0)
    o_ref[0] = out


def _decode_attention(page_table, q, k_new, v_new, lft_new, cache_kt, cache_vt, cache_lft, *, n_heads):
    b, nq, d = q.shape
    n_pages = page_table.shape[1]
    page = cache_kt.shape[2]
    rows = n_heads * nq
    seq_spec = pl.BlockSpec((1, nq, d), lambda i, pt: (i, 0, 0))
    grid_spec = pltpu.PrefetchScalarGridSpec(
        num_scalar_prefetch=1,
        grid=(b,),
        in_specs=[
            seq_spec, seq_spec, seq_spec,
            pl.BlockSpec((1, n_heads, page), lambda i, pt: (i, 0, 0)),
            pl.BlockSpec(memory_space=pl.ANY),
            pl.BlockSpec(memory_space=pl.ANY),
            pl.BlockSpec(memory_space=pl.ANY),
        ],
        out_specs=seq_spec,
        scratch_shapes=[
            pltpu.VMEM((N_SLOTS, d, page), F32),
            pltpu.VMEM((N_SLOTS, n_heads, page), F32),
            pltpu.SemaphoreType.DMA((N_SLOTS,)),
            pltpu.SemaphoreType.DMA((N_SLOTS,)),
            pltpu.VMEM((n_pages + 1, rows, page), F32),
            pltpu.VMEM((n_pages + 1, rows, page), BF16),
            pltpu.VMEM((rows, d), F32),
            pltpu.VMEM((rows, d), BF16),
        ],
    )
    return pl.pallas_call(
        functools.partial(_decode_kernel, n_heads=n_heads),
        grid_spec=grid_spec,
        out_shape=jax.ShapeDtypeStruct((b, nq, d), F32),
        compiler_params=_params("arbitrary"),
        name="decode_attention",
    )(page_table, q, k_new, v_new, lft_new, cache_kt, cache_vt, cache_lft)


def kernel(x_prompt, x_sample, cache_k, cache_v, cache_logf, state_rnn, state_conv, page_table,
           a_pre_norm, a_post_norm, a_w_in, a_conv_w, a_conv_b, a_w_ga, a_b_ga, a_w_gx, a_b_gx,
           a_lambda, a_w_out, kv_norm, w_kv, b_f, b_pre_norm, b_post_norm, b_w_in, b_w_out):
    n_a = a_w_in.shape[0]
    n_b = b_w_in.shape[0]
    d = x_prompt.shape[-1]
    n_heads = b_f.shape[0]
    hd = d // n_heads
    assert 2 * hd == LANES and n_heads <= LANES

    a_win = a_w_in.astype(BF16)
    a_wout = a_w_out.astype(BF16)
    a_wg = jnp.concatenate([a_w_ga, a_w_gx], axis=-1).astype(BF16)
    w_kv_t = w_kv.T.astype(BF16)
    wk = w_kv[:, :d].astype(BF16)
    wv = w_kv[:, d:2 * d].astype(BF16)
    wf = jnp.pad(w_kv[:, 2 * d:], ((0, 0), (0, LANES - n_heads))).astype(BF16)
    bfr = jnp.pad(b_f, (0, LANES - n_heads)).reshape(1, LANES)
    bfc = b_f.reshape(n_heads, 1)
    b_win = b_w_in.astype(BF16)
    b_wout = b_w_out.astype(BF16)
    b_win_t = jnp.swapaxes(b_w_in, 1, 2).astype(BF16)

    def a_layer(l, x, h0, c0, tt, nb):
        return _layer_a(x, h0, c0, a_pre_norm[l], a_post_norm[l], a_win[l], a_conv_w[l], a_conv_b[l],
                        a_wg[l], a_b_ga[l], a_b_gx[l], a_lambda[l], a_wout[l], tt=tt, nb=nb)

    bp, tp, _ = x_prompt.shape
    x = x_prompt
    rnn_p, conv_p = [], []
    for l in range(n_a):
        x, hl, cn = a_layer(l, x, jnp.zeros((bp, 1, d), F32), jnp.zeros((bp, CONV_W - 1, d), F32),
                            tt=_tile(tp, RGLRU_TILE), nb=1)
        rnn_p.append(hl.reshape(bp, d))
        conv_p.append(cn)
    tt = _tile(tp, PROMPT_TILE)
    kt_p, vt_p, lft_p, ct_p, kaug_p, vtb_p = _kv_prompt(x, kv_norm, wk, w_kv_t, wf, bfr, bfc,
                                                        tt=tt, n_heads=n_heads)
    for l in range(n_b):
        qt, gt = _qg_prompt(x, b_pre_norm[l], b_win_t[l], ct_p, tt=tt, n_heads=n_heads)
        ot = _prompt_attention(qt, kaug_p, vtb_p, blk=tt, hd=hd)
        x = _out_prompt(ot, gt, x, b_wout[l], b_post_norm[l], tt=tt)
    y_prompt = x
    k_p = jnp.transpose(kt_p.reshape(bp, n_heads, hd, tp), (0, 3, 1, 2))
    v_p = jnp.transpose(vt_p.reshape(bp, n_heads, hd, tp), (0, 3, 1, 2))
    lf_p = jnp.swapaxes(lft_p, 1, 2)

    bs, ts, _ = x_sample.shape
    rows_s = bs * ts
    n_pool, page = cache_k.shape[:2]
    x = jnp.swapaxes(x_sample, 0, 1).reshape(1, rows_s, d)
    rnn_s, conv_s = [], []
    for l in range(n_a):
        c0 = jnp.swapaxes(state_conv[l], 0, 1).reshape(1, (CONV_W - 1) * bs, d)
        x, hl, cn = a_layer(l, x, state_rnn[l].reshape(1, bs, d), c0, tt=ts, nb=bs)
        rnn_s.append(hl.reshape(bs, d))
        conv_s.append(jnp.swapaxes(cn.reshape(CONV_W - 1, bs, d), 0, 1))
    x = jnp.swapaxes(x.reshape(ts, bs, d), 0, 1).reshape(rows_s, d)
    k_s, v_s, lf_s = _kv_sample(x, kv_norm, wk, wv, wf, bfr, n_heads=n_heads)
    lft_new = jnp.pad(jnp.swapaxes(lf_s.reshape(bs, ts, n_heads), 1, 2),
                      ((0, 0), (0, 0), (0, page - ts)))
    cache_kt = jnp.transpose(cache_k, (0, 2, 3, 1)).reshape(n_pool, d, page)
    cache_vt = jnp.transpose(cache_v, (0, 2, 3, 1)).reshape(n_pool, d, page)
    cache_lft = jnp.swapaxes(cache_logf, 1, 2)
    for l in range(n_b):
        q, gate = _qg_sample(x, b_pre_norm[l], b_win[l], n_heads=n_heads)
        o = _decode_attention(page_table, q.reshape(bs, ts, d), k_s.reshape(bs, ts, d),
                              v_s.reshape(bs, ts, d), lft_new, cache_kt, cache_vt, cache_lft,
                              n_heads=n_heads)
        x = _out_sample(o.reshape(rows_s, d), gate, x, b_wout[l], b_post_norm[l])
    y_sample = x.reshape(bs, ts, d)

    return (y_prompt, y_sample, k_p, v_p, lf_p,
            jnp.stack(rnn_p), jnp.stack(conv_p),
            k_s.reshape(bs, ts, n_heads, hd), v_s.reshape(bs, ts, n_heads, hd),
            lf_s.reshape(bs, ts, n_heads),
            jnp.stack(rnn_s), jnp.stack(conv_s))
```

```python
import functools

import jax
import jax.numpy as jnp
from jax import lax
from jax.experimental import pallas as pl
from jax.experimental.pallas import tpu as pltpu

F32 = jnp.float32
BF16 = jnp.bfloat16

EPS = 1e-6
RGLRU_C = 8.0
CONV_W = 4
LANES = 128
SUBLANES = 8
BF16_ROWS = 16
N_AUG = 3
LOG2E = 1.4426950408889634
VMEM_LIMIT = 56 * 1024 * 1024
RGLRU_TILE = 256
PROMPT_TILE = 512


def _tile(n, want):
    t = min(n, want)
    while n % t or t % LANES:
        t -= LANES
    return t

_NT = (((1,), (1,)), ((), ()))
_TN = (((0,), (0,)), ((), ()))


def _rms(x, g):
    ms = jnp.mean(x * x, axis=-1, keepdims=True)
    return x * lax.rsqrt(ms + EPS) * g


def _softplus(x):
    return jnp.maximum(x, 0.0) + jnp.log1p(jnp.exp(-jnp.abs(x)))


def _log_sigmoid(x):
    return jnp.minimum(x, 0.0) - jnp.log1p(jnp.exp(-jnp.abs(x)))


def _expm1_of_square(x, u):
    d = u - 1.0
    edge = (d == 0.0) | (d == -1.0)
    r = d * x / jnp.log(jnp.where(edge, 0.5, u))
    return jnp.where(d == 0.0, x, jnp.where(d == -1.0, -1.0, r))


def _silu(x):
    return x * jax.nn.sigmoid(x)


def _split3(x):
    hi = x.astype(BF16).astype(F32)
    r1 = x - hi
    mid = r1.astype(BF16).astype(F32)
    lo = (r1 - mid).astype(BF16).astype(F32)
    return hi, mid, lo


def _exact_dot(ones_lhs, x):
    hi, mid, lo = _split3(x)
    d = lambda p: jnp.dot(ones_lhs, p.astype(BF16), preferred_element_type=F32)
    return d(hi) + d(mid) + d(lo)


def _exact_dot_rhs(x, ones_rhs):
    hi, mid, lo = _split3(x)
    d = lambda p: jnp.dot(p.astype(BF16), ones_rhs, preferred_element_type=F32)
    return d(hi) + d(mid) + d(lo)


def _const_spec(shape):
    nd = len(shape)
    return pl.BlockSpec(shape, lambda *_: (0,) * nd)


def _params(*sem):
    return pltpu.CompilerParams(dimension_semantics=sem, vmem_limit_bytes=VMEM_LIMIT)


SEQS_PER_STEP = 2


def _rglru_gates(uc, wg_ref, bga_ref, bgx_ref, lam_ref, a_s, b_s):
    d = uc.shape[-1]
    nblk = wg_ref.shape[0]
    blk = d // nblk
    rate = (-RGLRU_C) * _softplus(-lam_ref[...])
    ucb = uc.astype(BF16)
    for n in range(nblk):
        sl = slice(n * blk, (n + 1) * blk)
        gz = jnp.dot(ucb[:, sl], wg_ref[n], preferred_element_type=F32)
        r = jax.nn.sigmoid(gz[:, :blk] + bga_ref[:, sl])
        i = jax.nn.sigmoid(gz[:, blk:] + bgx_ref[:, sl])
        log_a = r * rate[:, sl]
        a = jnp.exp(log_a)
        a_s[:, sl] = a
        b_s[:, sl] = jnp.sqrt(-_expm1_of_square(2.0 * log_a, a * a)) * (i * uc[:, sl])


def _layer_a_seq_kernel(x_ref, h0_ref, c0_ref, pre_ref, post_ref, win_ref, cw_ref, cb_ref,
                        wg_ref, bga_ref, bgx_ref, lam_ref, wout_ref,
                        y_ref, hl_ref, cn_ref,
                        ubuf, a_s, b_s, g_s, hcar, tail_s, perm_s, *, tt):
    t = pl.program_id(1)
    n_seq, _, d = x_ref.shape
    seg = tt // SUBLANES
    n_tail = CONV_W - 1
    head = n_tail * SUBLANES

    @pl.when(t == 0)
    def _():
        hcar[...] = h0_ref[...]
        tail_s[:, 0:n_tail, :] = c0_ref[...]
        r = lax.broadcasted_iota(jnp.int32, (tt, tt), 0)
        c = lax.broadcasted_iota(jnp.int32, (tt, tt), 1)
        time_of = lambda i: (i & (SUBLANES - 1)) * seg + (i >> 3)
        perm_s[0] = jnp.where(c == time_of(r), 1.0, 0.0).astype(BF16)
        perm_s[1] = jnp.where(r == time_of(c), 1.0, 0.0).astype(BF16)

    for b in range(n_seq):
        xn = _rms(x_ref[b], pre_ref[...]).astype(BF16)
        xp = jnp.dot(perm_s[0], xn, preferred_element_type=F32).astype(BF16)
        z = jnp.dot(xp, win_ref[...], preferred_element_type=F32)
        u = z[:, :d]
        g_s[b] = z[:, d:]

        ubuf[b, head:head + tt, :] = u
        sub = lax.broadcasted_iota(jnp.int32, (SUBLANES, d), 0)
        for j in range(n_tail):
            grp = u[(seg - n_tail + j) * SUBLANES:(seg - n_tail + j + 1) * SUBLANES, :]
            ubuf[b, j * SUBLANES:(j + 1) * SUBLANES, :] = jnp.where(
                sub == 0, tail_s[b, j:j + 1, :], pltpu.roll(grp, 1, 0))
            tail_s[b, j:j + 1, :] = grp[SUBLANES - 1:SUBLANES, :]
        cn_ref[b] = tail_s[b, 0:n_tail, :]

        cw = cw_ref[...]
        uc = u * cw[n_tail:CONV_W, :] + cb_ref[...]
        for j in range(n_tail):
            uc = uc + ubuf[b, j * SUBLANES:j * SUBLANES + tt, :] * cw[j:j + 1, :]

        _rglru_gates(uc, wg_ref, bga_ref, bgx_ref, lam_ref, a_s.at[b], b_s.at[b])

    def body(k, carry):
        rows = pl.ds(pl.multiple_of(k * SUBLANES, SUBLANES), SUBLANES)
        out = []
        for b in range(n_seq):
            h, prod = carry[b]
            a = a_s[b, rows, :]
            h = a * h + b_s[b, rows, :]
            prod = prod * a
            b_s[b, rows, :] = h
            a_s[b, rows, :] = prod
            out.append((h, prod))
        return tuple(out)
    init = (jnp.zeros((SUBLANES, d), F32), jnp.ones((SUBLANES, d), F32))
    ends = lax.fori_loop(0, seg, body, (init,) * n_seq)

    for b in range(n_seq):
        h_end, a_end = ends[b]
        c = hcar[b]
        enter = []
        for s in range(SUBLANES):
            enter.append(c)
            c = h_end[s:s + 1, :] + a_end[s:s + 1, :] * c
        hcar[b] = c
        hl_ref[b] = c
        enter = jnp.concatenate(enter, axis=0)[None]

        h = (b_s[b].reshape(seg, SUBLANES, d) + a_s[b].reshape(seg, SUBLANES, d) * enter)
        hg = (h.reshape(tt, d) * _silu(g_s[b])).astype(BF16)
        hg = jnp.dot(perm_s[1], hg, preferred_element_type=F32).astype(BF16)
        y = jnp.dot(hg, wout_ref[...], preferred_element_type=F32)
        y_ref[b] = x_ref[b] + _rms(y, post_ref[...])


def _layer_a_kernel(x_ref, h0_ref, c0_ref, pre_ref, post_ref, win_ref, cw_ref, cb_ref,
                    wg_ref, bga_ref, bgx_ref, lam_ref, wout_ref,
                    y_ref, hl_ref, cn_ref,
                    ubuf, a_s, b_s, hcar, *, tt, nb, off):
    t = pl.program_id(1)
    rows = tt * nb
    d = x_ref.shape[-1]
    tail = (CONV_W - 1) * nb

    @pl.when(t == 0)
    def _():
        hcar[...] = h0_ref[0]
        ubuf[off - tail:off, :] = c0_ref[0]

    x = x_ref[0]
    xn = _rms(x, pre_ref[...])
    z = jnp.dot(xn.astype(BF16), win_ref[...], preferred_element_type=F32)
    u = z[:, :d]
    g = z[:, d:]

    ubuf[off:off + rows, :] = u
    cw = cw_ref[...]
    uc = u * cw[CONV_W - 1:CONV_W, :] + cb_ref[...]
    for j in range(CONV_W - 1):
        lo = off - (CONV_W - 1 - j) * nb
        uc = uc + ubuf[lo:lo + rows, :] * cw[j:j + 1, :]
    ubuf[off - tail:off, :] = ubuf[off + rows - tail:off + rows, :]

    _rglru_gates(uc, wg_ref, bga_ref, bgx_ref, lam_ref, a_s, b_s)

    h = hcar[...]
    for s in range(tt):
        sl = slice(s * nb, (s + 1) * nb)
        h = a_s[sl, :] * h + b_s[sl, :]
        b_s[sl, :] = h
    hcar[...] = h
    hl_ref[0] = h
    cn_ref[0] = ubuf[off - tail:off, :]

    hg = b_s[...] * _silu(g)
    y = jnp.dot(hg.astype(BF16), wout_ref[...], preferred_element_type=F32)
    y_ref[0] = x + _rms(y, post_ref[...])


def _layer_a(x, h0, c0, pre, post, win, cw, cb, wg, bga, bgx, lam, wout, *, tt, nb):
    g_, rows_total, d = x.shape
    rows = tt * nb
    nt = rows_total // rows
    assert nt * rows == rows_total and tt >= CONV_W - 1
    assert (nb == 1 and tt % SUBLANES == 0) or nb % SUBLANES == 0
    tail = (CONV_W - 1) * nb
    off = -(-tail // SUBLANES) * SUBLANES
    row = lambda v: v.reshape(1, d)
    if nb == 1:
        assert tt % (SUBLANES * SUBLANES) == 0 and tt // SUBLANES > CONV_W - 1
        sps = SEQS_PER_STEP if g_ % SEQS_PER_STEP == 0 else 1
        kern = functools.partial(_layer_a_seq_kernel, tt=tt)
        scratch = [
            pltpu.VMEM((sps, tail * SUBLANES + rows, d), F32),
            pltpu.VMEM((sps, rows, d), F32),
            pltpu.VMEM((sps, rows, d), F32),
            pltpu.VMEM((sps, rows, d), F32),
            pltpu.VMEM((sps, 1, d), F32),
            pltpu.VMEM((sps, SUBLANES, d), F32),
            pltpu.VMEM((2, tt, tt), BF16),
        ]
    else:
        sps = 1
        kern = functools.partial(_layer_a_kernel, tt=tt, nb=nb, off=off)
        scratch = [
            pltpu.VMEM((off + rows, d), F32),
            pltpu.VMEM((rows, d), F32),
            pltpu.VMEM((rows, d), F32),
            pltpu.VMEM((nb, d), F32),
        ]
    return pl.pallas_call(
        kern,
        grid=(g_ // sps, nt),
        in_specs=[
            pl.BlockSpec((sps, rows, d), lambda g, t: (g, t, 0)),
            pl.BlockSpec((sps, nb, d), lambda g, t: (g, 0, 0)),
            pl.BlockSpec((sps, tail, d), lambda g, t: (g, 0, 0)),
            _const_spec((1, d)), _const_spec((1, d)),
            _const_spec(win.shape), _const_spec(cw.shape), _const_spec((1, d)),
            _const_spec(wg.shape), _const_spec((1, d)), _const_spec((1, d)), _const_spec((1, d)),
            _const_spec(wout.shape),
        ],
        out_specs=[
            pl.BlockSpec((sps, rows, d), lambda g, t: (g, t, 0)),
            pl.BlockSpec((sps, nb, d), lambda g, t: (g, 0, 0)),
            pl.BlockSpec((sps, tail, d), lambda g, t: (g, 0, 0)),
        ],
        out_shape=[
            jax.ShapeDtypeStruct((g_, rows_total, d), F32),
            jax.ShapeDtypeStruct((g_, nb, d), F32),
            jax.ShapeDtypeStruct((g_, tail, d), F32),
        ],
        scratch_shapes=scratch,
        compiler_params=_params("parallel", "arbitrary"),
        name="rglru_layer",
    )(x, h0, c0, row(pre), row(post), win, cw, row(cb), wg, row(bga), row(bgx), row(lam), wout)


def _aug_base(h, hd):
    upper = ((h * hd) % LANES) != 0
    return (0 if upper else hd), upper


def _key_block(k, col, h, hd):
    rows = k.shape[0]
    pair = (h * hd) // LANES
    blk = k[:, pair * LANES:(pair + 1) * LANES]
    a0, upper = _aug_base(h, hd)
    lane = lax.broadcasted_iota(jnp.int32, (rows, LANES), 1)
    data = (lane >= hd) if upper else (lane < hd)
    hi, mid, lo = _split3(-LOG2E * col)
    v0 = a0 + N_AUG
    aug = jnp.where(lane == v0, hi,
                    jnp.where(lane == v0 + 1, mid,
                              jnp.where(lane == v0 + 2, lo,
                                        jnp.where((lane >= a0) & (lane < v0), 1.0, 0.0))))
    return jnp.where(data, blk, aug).astype(BF16)


def _query_block_t(qt, row, h, hd):
    cols = qt.shape[1]
    a0, upper = _aug_base(h, hd)
    sub = lax.broadcasted_iota(jnp.int32, (SUBLANES, cols), 0)
    hi, mid, lo = _split3(LOG2E * row)
    aug = jnp.where(sub == 0, hi,
                    jnp.where(sub == 1, mid,
                              jnp.where(sub == 2, lo,
                                        jnp.where(sub < 2 * N_AUG, 1.0, 0.0))))
    pad = jnp.zeros((LANES - hd - SUBLANES, cols), F32)
    data = qt[h * hd:(h + 1) * hd, :]
    parts = [aug, pad, data] if upper else [data, aug, pad]
    return jnp.concatenate(parts, axis=0).astype(BF16)


def _kv_prompt_kernel(x_ref, g_ref, wk_ref, wkvt_ref, wf_ref, bfr_ref, bfc_ref,
                      kt_ref, vt_ref, lft_ref, ct_ref, kp_ref, vtb_ref, carry_r, carry_c,
                      *, n_heads):
    x = x_ref[0]
    tt, d = x.shape
    hd = d // n_heads
    xn = _rms(x, g_ref[...]).astype(BF16)
    nt_dot = lambda lo, hi: lax.dot_general(wkvt_ref[lo:hi, :], xn, _NT,
                                            preferred_element_type=F32)
    kt = nt_dot(0, d)
    vt = nt_dot(d, 2 * d)
    k = jnp.dot(xn, wk_ref[...], preferred_element_type=F32)
    lf = _log_sigmoid(jnp.dot(xn, wf_ref[...], preferred_element_type=F32) + bfr_ref[...])
    lft = _log_sigmoid(nt_dot(2 * d, 2 * d + n_heads) + bfc_ref[...])
    kt_ref[0] = kt
    vt_ref[0] = vt
    vtb_ref[0] = vt.astype(BF16)
    lft_ref[0] = lft

    @pl.when(pl.program_id(1) == 0)
    def _():
        carry_r[...] = jnp.zeros_like(carry_r)
        carry_c[...] = jnp.zeros_like(carry_c)

    r = lax.broadcasted_iota(jnp.int32, (tt, tt), 0)
    c = lax.broadcasted_iota(jnp.int32, (tt, tt), 1)
    cs = _exact_dot(jnp.where(r >= c, 1.0, 0.0).astype(BF16), lf) + carry_r[...]
    cst = _exact_dot_rhs(lft, jnp.where(r <= c, 1.0, 0.0).astype(BF16)) + carry_c[...]
    carry_r[...] = cs[tt - 1:tt, :]
    carry_c[...] = cst[:, tt - 1:tt]
    ct_ref[0] = cst
    for h in range(n_heads):
        kp_ref[0, h] = _key_block(k, cs[:, h:h + 1], h, hd)


def _kv_prompt(x, g, wk, wkvt, wf, bfr, bfc, *, tt, n_heads):
    b, t, d = x.shape
    nt = t // tt
    assert nt * tt == t
    feat = lambda n, dt: (pl.BlockSpec((1, n, tt), lambda i, j: (i, 0, j)),
                          jax.ShapeDtypeStruct((b, n, t), dt))
    outs = [feat(d, F32), feat(d, F32), feat(n_heads, F32), feat(n_heads, F32),
            (pl.BlockSpec((1, n_heads, tt, LANES), lambda i, j: (i, 0, j, 0)),
             jax.ShapeDtypeStruct((b, n_heads, t, LANES), BF16)),
            feat(d, BF16)]
    return pl.pallas_call(
        functools.partial(_kv_prompt_kernel, n_heads=n_heads),
        grid=(b, nt),
        in_specs=[pl.BlockSpec((1, tt, d), lambda i, j: (i, j, 0)), _const_spec((1, d)),
                  _const_spec(wk.shape), _const_spec(wkvt.shape), _const_spec(wf.shape),
                  _const_spec(bfr.shape), _const_spec(bfc.shape)],
        out_specs=[o[0] for o in outs],
        out_shape=[o[1] for o in outs],
        scratch_shapes=[pltpu.VMEM((1, LANES), F32), pltpu.VMEM((n_heads, 1), F32)],
        compiler_params=_params("parallel", "arbitrary"),
        name="kv_prompt",
    )(x, g.reshape(1, d), wk, wkvt, wf, bfr, bfc)


def _qg_prompt_kernel(x_ref, g_ref, wt_ref, ct_ref, qt_ref, gt_ref, *, n_heads):
    x = x_ref[0]
    d = x.shape[-1]
    hd = d // n_heads
    xn = _rms(x, g_ref[...]).astype(BF16)
    zt = lax.dot_general(wt_ref[...], xn, _NT, preferred_element_type=F32)
    gt_ref[0] = zt[d:, :].astype(BF16)
    qt = zt[:d, :] * (hd ** -0.5 * LOG2E)
    ct = ct_ref[0]
    for h in range(n_heads):
        qt_ref[0, h] = _query_block_t(qt, ct[h:h + 1, :], h, hd)


def _qg_prompt(x, g, wt, ct, *, tt, n_heads):
    b, t, d = x.shape
    nt = t // tt
    assert nt * tt == t
    return pl.pallas_call(
        functools.partial(_qg_prompt_kernel, n_heads=n_heads),
        grid=(b, nt),
        in_specs=[pl.BlockSpec((1, tt, d), lambda i, j: (i, j, 0)), _const_spec((1, d)),
                  _const_spec(wt.shape),
                  pl.BlockSpec((1, n_heads, tt), lambda i, j: (i, 0, j))],
        out_specs=[pl.BlockSpec((1, n_heads, LANES, tt), lambda i, j: (i, 0, 0, j)),
                   pl.BlockSpec((1, d, tt), lambda i, j: (i, 0, j))],
        out_shape=[jax.ShapeDtypeStruct((b, n_heads, LANES, t), BF16),
                   jax.ShapeDtypeStruct((b, d, t), BF16)],
        compiler_params=_params("parallel", "parallel"),
        name="qg_prompt",
    )(x, g.reshape(1, d), wt, ct)


ATTN_FULL2, ATTN_FULL_DIAG, ATTN_DIAG = 0, 1, 2


def _attn_kernel(qi_tab, kj_tab, kind_tab, q_ref, k_ref, v_ref, o_ref, m_s, l_s, al_s, mc_s, acc_s,
                 *bufs, hd):
    st_bufs, pt_bufs = bufs[:4], bufs[4:]
    step = pl.program_id(1)
    kj = kj_tab[step]
    kind = kind_tab[step]
    n_heads, _, tq = q_ref.shape[1:]
    tk = tq
    n_pairs = n_heads // 2

    @pl.when(kj == 0)
    def _():
        m_s[...] = jnp.full(m_s.shape, -jnp.inf, F32)
        l_s[...] = jnp.zeros_like(l_s)
        acc_s[...] = jnp.zeros_like(acc_s)

    def run(diag_flags):
        items = [(kb, pair) for kb in range(len(diag_flags)) for pair in range(n_pairs)]

        def scores(n, s):
            kb, pair = items[n]
            for j in range(2):
                h = 2 * pair + j
                st = jnp.dot(k_ref[0, h, kb * tk:(kb + 1) * tk, :], q_ref[0, h],
                             preferred_element_type=F32)
                if diag_flags[kb]:
                    keep = (lax.broadcasted_iota(jnp.int32, (tk, tq), 0)
                            <= lax.broadcasted_iota(jnp.int32, (tk, tq), 1))
                    st = jnp.where(keep, st, -jnp.inf)
                st_bufs[2 * s + j][...] = st
                mc_s[2 * s + j] = jnp.max(st, axis=0, keepdims=True)

        def softmax(n, s):
            for j in range(2):
                h = 2 * items[n][1] + j
                m_prev = m_s[h]
                m_new = jnp.maximum(m_prev, mc_s[2 * s + j])
                m_s[h] = m_new
                al_s[h] = jnp.exp2(m_prev - m_new)
                pt_bufs[2 * s + j][...] = jnp.exp2(st_bufs[2 * s + j][...] - m_new).astype(BF16)

        ones_rows = jnp.ones((BF16_ROWS, tk), BF16)

        def values(n, s):
            kb, pair = items[n]
            for j in range(2):
                h = 2 * pair + j
                rows = slice(h * hd, (h + 1) * hd)
                pv = jnp.dot(jnp.concatenate([v_ref[0, rows, kb * tk:(kb + 1) * tk], ones_rows],
                                             axis=0),
                             pt_bufs[2 * s + j][...], preferred_element_type=F32)
                alpha = al_s[h]
                acc_s[rows, :] = acc_s[rows, :] * alpha + pv[:hd, :]
                l_s[h] = alpha * l_s[h] + pv[hd:hd + 1, :]

        for t in range(len(items) + 2):
            if t < len(items):
                scores(t, t % 2)
            if 1 <= t <= len(items):
                softmax(t - 1, (t - 1) % 2)
            if t >= 2:
                values(t - 2, t % 2)

    def finish():
        for h in range(n_heads):
            rows = slice(h * hd, (h + 1) * hd)
            o_ref[0, rows, :] = (acc_s[rows, :] * (1.0 / l_s[h])).astype(o_ref.dtype)

    @pl.when(kind == ATTN_FULL2)
    def _():
        run((False, False))

    @pl.when(kind == ATTN_FULL_DIAG)
    def _():
        run((False, True))
        finish()

    @pl.when(kind == ATTN_DIAG)
    def _():
        run((True,))
        finish()


def _prompt_attention(qt, kp, vtb, *, blk, hd):
    b, n_heads, _, t = qt.shape
    d = vtb.shape[1]
    nq = t // blk
    assert nq * blk == t and nq % 2 == 0
    qi_l, kj_l, kind_l = [], [], []
    for i in range(nq):
        for j in range(i // 2 + 1):
            qi_l.append(i)
            kj_l.append(j)
            kind_l.append(ATTN_FULL2 if 2 * j + 1 < i else
                          ATTN_FULL_DIAG if 2 * j + 1 == i else ATTN_DIAG)
    grid_spec = pltpu.PrefetchScalarGridSpec(
        num_scalar_prefetch=3,
        grid=(b, len(qi_l)),
        in_specs=[
            pl.BlockSpec((1, n_heads, LANES, blk), lambda i, s, qt_, kt_, kd_: (i, 0, 0, qt_[s])),
            pl.BlockSpec((1, n_heads, 2 * blk, LANES),
                         lambda i, s, qt_, kt_, kd_: (i, 0, kt_[s], 0)),
            pl.BlockSpec((1, d, 2 * blk), lambda i, s, qt_, kt_, kd_: (i, 0, kt_[s])),
        ],
        out_specs=pl.BlockSpec((1, d, blk), lambda i, s, qt_, kt_, kd_: (i, 0, qt_[s])),
        scratch_shapes=[
            pltpu.VMEM((n_heads, 1, blk), F32),
            pltpu.VMEM((n_heads, 1, blk), F32),
            pltpu.VMEM((n_heads, 1, blk), F32),
            pltpu.VMEM((4, 1, blk), F32),
            pltpu.VMEM((d, blk), F32),
        ] + [pltpu.VMEM((blk, blk), F32)] * 4 + [pltpu.VMEM((blk, blk), BF16)] * 4,
    )
    return pl.pallas_call(
        functools.partial(_attn_kernel, hd=hd),
        grid_spec=grid_spec,
        out_shape=jax.ShapeDtypeStruct((b, d, t), BF16),
        compiler_params=_params("parallel", "arbitrary"),
        name="prompt_attention",
    )(jnp.asarray(qi_l, jnp.int32), jnp.asarray(kj_l, jnp.int32), jnp.asarray(kind_l, jnp.int32),
      qt, kp, vtb)


def _out_prompt_kernel(ot_ref, gt_ref, x_ref, w_ref, post_ref, y_ref):
    hgt = ot_ref[0].astype(F32) * _silu(gt_ref[0].astype(F32))
    y = lax.dot_general(hgt.astype(BF16), w_ref[...], _TN, preferred_element_type=F32)
    y_ref[0] = x_ref[0] + _rms(y, post_ref[...])


def _out_prompt(ot, gt, x, wt, post, *, tt):
    b, t, d = x.shape
    nt = t // tt
    assert nt * tt == t
    row_spec = pl.BlockSpec((1, tt, d), lambda i, j: (i, j, 0))
    feat_spec = pl.BlockSpec((1, d, tt), lambda i, j: (i, 0, j))
    return pl.pallas_call(
        _out_prompt_kernel,
        grid=(b, nt),
        in_specs=[feat_spec, feat_spec, row_spec, _const_spec(wt.shape), _const_spec((1, d))],
        out_specs=row_spec,
        out_shape=jax.ShapeDtypeStruct((b, t, d), F32),
        compiler_params=_params("parallel", "parallel"),
        name="out_prompt",
    )(ot, gt, x, wt, post.reshape(1, d))


def _kv_sample_kernel(x_ref, g_ref, wk_ref, wv_ref, wf_ref, bfr_ref, k_ref, v_ref, lf_ref,
                      *, n_heads):
    xn = _rms(x_ref[...], g_ref[...]).astype(BF16)
    k_ref[...] = jnp.dot(xn, wk_ref[...], preferred_element_type=F32)
    v_ref[...] = jnp.dot(xn, wv_ref[...], preferred_element_type=F32)
    lf = _log_sigmoid(jnp.dot(xn, wf_ref[...], preferred_element_type=F32) + bfr_ref[...])
    lf_ref[...] = lf[:, :n_heads]


def _kv_sample(x, g, wk, wv, wf, bfr, *, n_heads):
    rows, d = x.shape
    return pl.pallas_call(
        functools.partial(_kv_sample_kernel, n_heads=n_heads),
        out_shape=[jax.ShapeDtypeStruct((rows, d), F32), jax.ShapeDtypeStruct((rows, d), F32),
                   jax.ShapeDtypeStruct((rows, n_heads), F32)],
        compiler_params=pltpu.CompilerParams(vmem_limit_bytes=VMEM_LIMIT),
        name="kv_sample",
    )(x, g.reshape(1, d), wk, wv, wf, bfr)


def _qg_sample_kernel(x_ref, g_ref, wt_ref, q_ref, gate_ref, *, n_heads):
    d = x_ref.shape[-1]
    xn = _rms(x_ref[...], g_ref[...]).astype(BF16)
    z = lax.dot_general(xn, wt_ref[...], _NT, preferred_element_type=F32)
    q_ref[...] = z[:, :d] * ((d // n_heads) ** -0.5)
    gate_ref[...] = z[:, d:]


def _qg_sample(x, g, w, *, n_heads):
    rows, d = x.shape
    return pl.pallas_call(
        functools.partial(_qg_sample_kernel, n_heads=n_heads),
        out_shape=[jax.ShapeDtypeStruct((rows, d), F32), jax.ShapeDtypeStruct((rows, d), F32)],
        compiler_params=pltpu.CompilerParams(vmem_limit_bytes=VMEM_LIMIT),
        name="qg_sample",
    )(x, g.reshape(1, d), w)


def _out_sample_kernel(o_ref, gate_ref, x_ref, w_ref, post_ref, y_ref):
    hg = o_ref[...] * _silu(gate_ref[...])
    y = jnp.dot(hg.astype(BF16), w_ref[...], preferred_element_type=F32)
    y_ref[...] = x_ref[...] + _rms(y, post_ref[...])


def _out_sample(o, gate, x, w, post):
    rows, d = x.shape
    return pl.pallas_call(
        _out_sample_kernel,
        out_shape=jax.ShapeDtypeStruct((rows, d), F32),
        compiler_params=pltpu.CompilerParams(vmem_limit_bytes=VMEM_LIMIT),
        name="out_sample",
    )(o, gate, x, w, post.reshape(1, d))


PAGE_GROUP = 4
RING_GROUPS = 8
N_SLOTS = RING_GROUPS * PAGE_GROUP


def _decode_kernel(pt_ref, q_ref, kn_ref, vn_ref, lfn_ref, kc_hbm, vc_hbm, lfc_hbm, o_ref,
                   kvbuf, lbuf, sem_kv, sem_l, s_ref, p_ref, acc_ref, qbd_ref, *, n_heads):
    seq = pl.program_id(0)
    n_seq = pl.num_programs(0)
    n_pages = pt_ref.shape[1]
    d, page = kvbuf.shape[1:]
    nq = q_ref.shape[1]
    hd = d // n_heads
    rows = n_heads * nq
    assert rows == LANES and page == LANES and n_pages % PAGE_GROUP == 0
    k_groups = n_pages // PAGE_GROUP
    n_groups = 2 * k_groups

    def kv_copy(src_hbm, page_id, slot):
        return pltpu.make_async_copy(src_hbm.at[page_id], kvbuf.at[slot], sem_kv.at[slot])

    def lf_copy(page_id, slot):
        return pltpu.make_async_copy(lfc_hbm.at[page_id], lbuf.at[slot], sem_l.at[slot])

    first = seq * n_groups

    def group_base(gid):
        return lax.rem(gid, RING_GROUPS) * PAGE_GROUP

    def start_group(gid):
        base = group_base(gid)
        sq = lax.div(gid, n_groups)
        grp = lax.rem(gid, n_groups)
        live = gid < n_seq * n_groups

        @pl.when(live & (grp < k_groups))
        def _():
            for j in range(PAGE_GROUP):
                pid = pt_ref[sq, grp * PAGE_GROUP + j]
                kv_copy(kc_hbm, pid, base + j).start()
                lf_copy(pid, base + j).start()

        @pl.when(live & (grp >= k_groups))
        def _():
            for j in range(PAGE_GROUP):
                pid = pt_ref[sq, (grp - k_groups) * PAGE_GROUP + j]
                kv_copy(vc_hbm, pid, base + j).start()

    @pl.when(seq == 0)
    def _():
        for g in range(RING_GROUPS):
            start_group(jnp.int32(g))

    q = q_ref[0]
    lane_head = lax.broadcasted_iota(jnp.int32, (nq, d), 1) // hd
    qbd_ref[...] = jnp.concatenate([jnp.where(lane_head == h, q, 0.0) for h in range(n_heads)],
                                   axis=0).astype(BF16)

    r = lax.broadcasted_iota(jnp.int32, (page, 2 * page), 0)
    c = lax.broadcasted_iota(jnp.int32, (page, 2 * page), 1)
    tri_ones = jnp.where((r <= c) | (c >= page), 1.0, 0.0).astype(BF16)

    def head_rows(ct):
        return jnp.concatenate(
            [jnp.broadcast_to(ct[h:h + 1, :], (nq, ct.shape[1])) for h in range(n_heads)], axis=0)

    def k_group(g, carry):
        gid = first + g
        base = group_base(gid)
        for j in range(PAGE_GROUP):
            kv_copy(kc_hbm, 0, base + j).wait()
            lf_copy(0, base + j).wait()
        for j in range(PAGE_GROUP):
            sc = jnp.dot(qbd_ref[...], kvbuf[base + j].astype(BF16), preferred_element_type=F32)
            cum = _exact_dot_rhs(lbuf[base + j], tri_ones)
            s_ref[g * PAGE_GROUP + j] = sc - head_rows(carry + cum[:, :page])
            carry = carry + cum[:, page:]
        start_group(gid + RING_GROUPS)
        return carry

    past_total = lax.fori_loop(0, k_groups, k_group, jnp.zeros((n_heads, page), F32))

    ct_new = past_total + _exact_dot_rhs(lfn_ref[0], tri_ones)[:, :page]
    kn = jnp.concatenate([kn_ref[0], jnp.zeros((page - nq, d), F32)], axis=0).astype(BF16)
    sc_new = lax.dot_general(qbd_ref[...], kn, _NT, preferred_element_type=F32)
    key_idx = lax.broadcasted_iota(jnp.int32, (rows, page), 1)
    q_idx = lax.rem(lax.broadcasted_iota(jnp.int32, (rows, page), 0), nq)
    ct_rows = head_rows(ct_new)
    cq = jnp.sum(jnp.where(key_idx == q_idx, ct_rows, 0.0), axis=1, keepdims=True)
    s_ref[n_pages] = jnp.where(key_idx <= q_idx, sc_new - ct_rows, -jnp.inf)

    def max_body(i, m):
        return jnp.maximum(m, s_ref[i])
    m_el = lax.fori_loop(0, n_pages + 1, max_body, jnp.full((rows, page), -jnp.inf, F32))
    m = jnp.max(m_el, axis=1, keepdims=True) + cq

    def p_body(i, l):
        pe = jnp.exp((s_ref[i] + cq) - m)
        p_ref[i] = pe.astype(BF16)
        return l + pe
    l_el = lax.fori_loop(0, n_pages + 1, p_body, jnp.zeros((rows, page), F32))
    l = jnp.sum(l_el, axis=1, keepdims=True)

    vn = jnp.concatenate([vn_ref[0], jnp.zeros((page - nq, d), F32)], axis=0).astype(BF16)
    acc_ref[...] = jnp.dot(p_ref[n_pages], vn, preferred_element_type=F32)

    def v_group(g, carry):
        gid = first + k_groups + g
        base = group_base(gid)
        for j in range(PAGE_GROUP):
            kv_copy(vc_hbm, 0, base + j).wait()
        upd = None
        for j in range(PAGE_GROUP):
            pv = lax.dot_general(p_ref[g * PAGE_GROUP + j], kvbuf[base + j].astype(BF16), _NT,
                                 preferred_element_type=F32)
            upd = pv if upd is None else upd + pv
        acc_ref[...] += upd
        start_group(gid + RING_GROUPS)
        return carry

    lax.fori_loop(0, k_groups, v_group, 0)

    acc = acc_ref[...] * (1.0 / l)
    out = jnp.zeros((nq, d), F32)
    for h in range(n_heads):
        out = out + jnp.where(lane_head == h, acc[h * nq:(h + 1) * nq, :], 0.0)
    o_ref[0] = out


def _decode_attention(page_table, q, k_new, v_new, lft_new, cache_kt, cache_vt, cache_lft, *, n_heads):
    b, nq, d = q.shape
    n_pages = page_table.shape[1]
    page = cache_kt.shape[2]
    rows = n_heads * nq
    seq_spec = pl.BlockSpec((1, nq, d), lambda i, pt: (i, 0, 0))
    grid_spec = pltpu.PrefetchScalarGridSpec(
        num_scalar_prefetch=1,
        grid=(b,),
        in_specs=[
            seq_spec, seq_spec, seq_spec,
            pl.BlockSpec((1, n_heads, page), lambda i, pt: (i, 0, 0)),
            pl.BlockSpec(memory_space=pl.ANY),
            pl.BlockSpec(memory_space=pl.ANY),
            pl.BlockSpec(memory_space=pl.ANY),
        ],
        out_specs=seq_spec,
        scratch_shapes=[
            pltpu.VMEM((N_SLOTS, d, page), F32),
            pltpu.VMEM((N_SLOTS, n_heads, page), F32),
            pltpu.SemaphoreType.DMA((N_SLOTS,)),
            pltpu.SemaphoreType.DMA((N_SLOTS,)),
            pltpu.VMEM((n_pages + 1, rows, page), F32),
            pltpu.VMEM((n_pages + 1, rows, page), BF16),
            pltpu.VMEM((rows, d), F32),
            pltpu.VMEM((rows, d), BF16),
        ],
    )
    return pl.pallas_call(
        functools.partial(_decode_kernel, n_heads=n_heads),
        grid_spec=grid_spec,
        out_shape=jax.ShapeDtypeStruct((b, nq, d), F32),
        compiler_params=_params("arbitrary"),
        name="decode_attention",
    )(page_table, q, k_new, v_new, lft_new, cache_kt, cache_vt, cache_lft)


def kernel(x_prompt, x_sample, cache_k, cache_v, cache_logf, state_rnn, state_conv, page_table,
           a_pre_norm, a_post_norm, a_w_in, a_conv_w, a_conv_b, a_w_ga, a_b_ga, a_w_gx, a_b_gx,
           a_lambda, a_w_out, kv_norm, w_kv, b_f, b_pre_norm, b_post_norm, b_w_in, b_w_out):
    n_a = a_w_in.shape[0]
    n_b = b_w_in.shape[0]
    d = x_prompt.shape[-1]
    n_heads = b_f.shape[0]
    hd = d // n_heads
    assert 2 * hd == LANES and n_heads <= LANES

    a_win = a_w_in.astype(BF16)
    a_wout = a_w_out.astype(BF16)
    a_wg = jnp.concatenate([a_w_ga, a_w_gx], axis=-1).astype(BF16)
    w_kv_t = w_kv.T.astype(BF16)
    wk = w_kv[:, :d].astype(BF16)
    wv = w_kv[:, d:2 * d].astype(BF16)
    wf = jnp.pad(w_kv[:, 2 * d:], ((0, 0), (0, LANES - n_heads))).astype(BF16)
    bfr = jnp.pad(b_f, (0, LANES - n_heads)).reshape(1, LANES)
    bfc = b_f.reshape(n_heads, 1)
    b_wout = b_w_out.astype(BF16)
    b_win_t = jnp.swapaxes(b_w_in, 1, 2).astype(BF16)

    def a_layer(l, x, h0, c0, tt, nb):
        return _layer_a(x, h0, c0, a_pre_norm[l], a_post_norm[l], a_win[l], a_conv_w[l], a_conv_b[l],
                        a_wg[l], a_b_ga[l], a_b_gx[l], a_lambda[l], a_wout[l], tt=tt, nb=nb)

    bp, tp, _ = x_prompt.shape
    x = x_prompt
    rnn_p, conv_p = [], []
    for l in range(n_a):
        x, hl, cn = a_layer(l, x, jnp.zeros((bp, 1, d), F32), jnp.zeros((bp, CONV_W - 1, d), F32),
                            tt=_tile(tp, RGLRU_TILE), nb=1)
        rnn_p.append(hl.reshape(bp, d))
        conv_p.append(cn)
    tt = _tile(tp, PROMPT_TILE)
    kt_p, vt_p, lft_p, ct_p, kaug_p, vtb_p = _kv_prompt(x, kv_norm, wk, w_kv_t, wf, bfr, bfc,
                                                        tt=tt, n_heads=n_heads)
    for l in range(n_b):
        qt, gt = _qg_prompt(x, b_pre_norm[l], b_win_t[l], ct_p, tt=tt, n_heads=n_heads)
        ot = _prompt_attention(qt, kaug_p, vtb_p, blk=tt, hd=hd)
        x = _out_prompt(ot, gt, x, b_wout[l], b_post_norm[l], tt=tt)
    y_prompt = x
    k_p = jnp.transpose(kt_p.reshape(bp, n_heads, hd, tp), (0, 3, 1, 2))
    v_p = jnp.transpose(vt_p.reshape(bp, n_heads, hd, tp), (0, 3, 1, 2))
    lf_p = jnp.swapaxes(lft_p, 1, 2)

    bs, ts, _ = x_sample.shape
    rows_s = bs * ts
    n_pool, page = cache_k.shape[:2]
    x = jnp.swapaxes(x_sample, 0, 1).reshape(1, rows_s, d)
    rnn_s, conv_s = [], []
    for l in range(n_a):
        c0 = jnp.swapaxes(state_conv[l], 0, 1).reshape(1, (CONV_W - 1) * bs, d)
        x, hl, cn = a_layer(l, x, state_rnn[l].reshape(1, bs, d), c0, tt=ts, nb=bs)
        rnn_s.append(hl.reshape(bs, d))
        conv_s.append(jnp.swapaxes(cn.reshape(CONV_W - 1, bs, d), 0, 1))
    x = jnp.swapaxes(x.reshape(ts, bs, d), 0, 1).reshape(rows_s, d)
    k_s, v_s, lf_s = _kv_sample(x, kv_norm, wk, wv, wf, bfr, n_heads=n_heads)
    lft_new = jnp.pad(jnp.swapaxes(lf_s.reshape(bs, ts, n_heads), 1, 2),
                      ((0, 0), (0, 0), (0, page - ts)))
    cache_kt = jnp.transpose(cache_k, (0, 2, 3, 1)).reshape(n_pool, d, page)
    cache_vt = jnp.transpose(cache_v, (0, 2, 3, 1)).reshape(n_pool, d, page)
    cache_lft = jnp.swapaxes(cache_logf, 1, 2)
    for l in range(n_b):
        q, gate = _qg_sample(x, b_pre_norm[l], b_win_t[l], n_heads=n_heads)
        o = _decode_attention(page_table, q.reshape(bs, ts, d), k_s.reshape(bs, ts, d),
                              v_s.reshape(bs, ts, d), lft_new, cache_kt, cache_vt, cache_lft,
                              n_heads=n_heads)
        x = _out_sample(o.reshape(rows_s, d), gate, x, b_wout[l], b_post_norm[l])
    y_sample = x.reshape(bs, ts, d)

    return (y_prompt, y_sample, k_p, v_p, lf_p,
            jnp.stack(rnn_p), jnp.stack(conv_p),
            k_s.reshape(bs, ts, n_heads, hd), v_s.reshape(bs, ts, n_heads, hd),
            lf_s.reshape(bs, ts, n_heads),
            jnp.stack(rnn_s), jnp.stack(conv_s))
```

```python
import functools

import jax
import jax.numpy as jnp
from jax import lax
from jax.experimental import pallas as pl
from jax.experimental.pallas import tpu as pltpu

F32 = jnp.float32
BF16 = jnp.bfloat16

EPS = 1e-6
RGLRU_C = 8.0
CONV_W = 4
LANES = 128
SUBLANES = 8
BF16_ROWS = 16
N_AUG = 3
LOG2E = 1.4426950408889634
VMEM_LIMIT = 56 * 1024 * 1024
RGLRU_TILE = 256
PROMPT_TILE = 512


def _tile(n, want):
    t = min(n, want)
    while n % t or t % LANES:
        t -= LANES
    return t

_NT = (((1,), (1,)), ((), ()))
_TN = (((0,), (0,)), ((), ()))


def _rms(x, g):
    ms = jnp.mean(x * x, axis=-1, keepdims=True)
    return x * lax.rsqrt(ms + EPS) * g


def _softplus(x):
    return jnp.maximum(x, 0.0) + jnp.log1p(jnp.exp(-jnp.abs(x)))


def _log_sigmoid(x):
    return jnp.minimum(x, 0.0) - jnp.log1p(jnp.exp(-jnp.abs(x)))


def _expm1_of_square(x, u):
    d = u - 1.0
    edge = (d == 0.0) | (d == -1.0)
    r = d * x / jnp.log(jnp.where(edge, 0.5, u))
    return jnp.where(d == 0.0, x, jnp.where(d == -1.0, -1.0, r))


def _silu(x):
    return x * jax.nn.sigmoid(x)


def _split3(x):
    hi = x.astype(BF16).astype(F32)
    r1 = x - hi
    mid = r1.astype(BF16).astype(F32)
    lo = (r1 - mid).astype(BF16).astype(F32)
    return hi, mid, lo


def _exact_dot(ones_lhs, x):
    hi, mid, lo = _split3(x)
    d = lambda p: jnp.dot(ones_lhs, p.astype(BF16), preferred_element_type=F32)
    return d(hi) + d(mid) + d(lo)


def _exact_dot_rhs(x, ones_rhs):
    hi, mid, lo = _split3(x)
    d = lambda p: jnp.dot(p.astype(BF16), ones_rhs, preferred_element_type=F32)
    return d(hi) + d(mid) + d(lo)


def _const_spec(shape):
    nd = len(shape)
    return pl.BlockSpec(shape, lambda *_: (0,) * nd)


def _params(*sem):
    return pltpu.CompilerParams(dimension_semantics=sem, vmem_limit_bytes=VMEM_LIMIT)


SEQS_PER_STEP = 2


def _rglru_gates(uc, wg_ref, bga_ref, bgx_ref, lam_ref, a_s, b_s):
    d = uc.shape[-1]
    nblk = wg_ref.shape[0]
    blk = d // nblk
    rate = (-RGLRU_C) * _softplus(-lam_ref[...])
    ucb = uc.astype(BF16)
    for n in range(nblk):
        sl = slice(n * blk, (n + 1) * blk)
        gz = jnp.dot(ucb[:, sl], wg_ref[n], preferred_element_type=F32)
        r = jax.nn.sigmoid(gz[:, :blk] + bga_ref[:, sl])
        i = jax.nn.sigmoid(gz[:, blk:] + bgx_ref[:, sl])
        log_a = r * rate[:, sl]
        a = jnp.exp(log_a)
        a_s[:, sl] = a
        b_s[:, sl] = jnp.sqrt(-_expm1_of_square(2.0 * log_a, a * a)) * (i * uc[:, sl])


def _layer_a_seq_kernel(x_ref, h0_ref, c0_ref, pre_ref, post_ref, win_ref, cw_ref, cb_ref,
                        wg_ref, bga_ref, bgx_ref, lam_ref, wout_ref,
                        y_ref, hl_ref, cn_ref,
                        ubuf, a_s, b_s, g_s, hcar, tail_s, perm_s, *, tt):
    t = pl.program_id(1)
    n_seq, _, d = x_ref.shape
    seg = tt // SUBLANES
    n_tail = CONV_W - 1
    head = n_tail * SUBLANES

    @pl.when(t == 0)
    def _():
        hcar[...] = h0_ref[...]
        tail_s[:, 0:n_tail, :] = c0_ref[...]
        r = lax.broadcasted_iota(jnp.int32, (tt, tt), 0)
        c = lax.broadcasted_iota(jnp.int32, (tt, tt), 1)
        time_of = lambda i: (i & (SUBLANES - 1)) * seg + (i >> 3)
        perm_s[0] = jnp.where(c == time_of(r), 1.0, 0.0).astype(BF16)
        perm_s[1] = jnp.where(r == time_of(c), 1.0, 0.0).astype(BF16)

    for b in range(n_seq):
        xn = _rms(x_ref[b], pre_ref[...]).astype(BF16)
        xp = jnp.dot(perm_s[0], xn, preferred_element_type=F32).astype(BF16)
        z = jnp.dot(xp, win_ref[...], preferred_element_type=F32)
        u = z[:, :d]
        g_s[b] = z[:, d:]

        ubuf[b, head:head + tt, :] = u
        sub = lax.broadcasted_iota(jnp.int32, (SUBLANES, d), 0)
        for j in range(n_tail):
            grp = u[(seg - n_tail + j) * SUBLANES:(seg - n_tail + j + 1) * SUBLANES, :]
            ubuf[b, j * SUBLANES:(j + 1) * SUBLANES, :] = jnp.where(
                sub == 0, tail_s[b, j:j + 1, :], pltpu.roll(grp, 1, 0))
            tail_s[b, j:j + 1, :] = grp[SUBLANES - 1:SUBLANES, :]
        cn_ref[b] = tail_s[b, 0:n_tail, :]

        cw = cw_ref[...]
        uc = u * cw[n_tail:CONV_W, :] + cb_ref[...]
        for j in range(n_tail):
            uc = uc + ubuf[b, j * SUBLANES:j * SUBLANES + tt, :] * cw[j:j + 1, :]

        _rglru_gates(uc, wg_ref, bga_ref, bgx_ref, lam_ref, a_s.at[b], b_s.at[b])

    def body(k, carry):
        rows = pl.ds(pl.multiple_of(k * SUBLANES, SUBLANES), SUBLANES)
        out = []
        for b in range(n_seq):
            h, prod = carry[b]
            a = a_s[b, rows, :]
            h = a * h + b_s[b, rows, :]
            prod = prod * a
            b_s[b, rows, :] = h
            a_s[b, rows, :] = prod
            out.append((h, prod))
        return tuple(out)
    init = (jnp.zeros((SUBLANES, d), F32), jnp.ones((SUBLANES, d), F32))
    ends = lax.fori_loop(0, seg, body, (init,) * n_seq)

    for b in range(n_seq):
        h_end, a_end = ends[b]
        c = hcar[b]
        enter = []
        for s in range(SUBLANES):
            enter.append(c)
            c = h_end[s:s + 1, :] + a_end[s:s + 1, :] * c
        hcar[b] = c
        hl_ref[b] = c
        enter = jnp.concatenate(enter, axis=0)[None]

        h = (b_s[b].reshape(seg, SUBLANES, d) + a_s[b].reshape(seg, SUBLANES, d) * enter)
        hg = (h.reshape(tt, d) * _silu(g_s[b])).astype(BF16)
        hg = jnp.dot(perm_s[1], hg, preferred_element_type=F32).astype(BF16)
        y = jnp.dot(hg, wout_ref[...], preferred_element_type=F32)
        y_ref[b] = x_ref[b] + _rms(y, post_ref[...])


def _layer_a_kernel(x_ref, h0_ref, c0_ref, pre_ref, post_ref, win_ref, cw_ref, cb_ref,
                    wg_ref, bga_ref, bgx_ref, lam_ref, wout_ref,
                    y_ref, hl_ref, cn_ref,
                    ubuf, a_s, b_s, hcar, *, tt, nb, off):
    t = pl.program_id(1)
    rows = tt * nb
    d = x_ref.shape[-1]
    tail = (CONV_W - 1) * nb

    @pl.when(t == 0)
    def _():
        hcar[...] = h0_ref[0]
        ubuf[off - tail:off, :] = c0_ref[0]

    x = x_ref[0]
    xn = _rms(x, pre_ref[...])
    z = jnp.dot(xn.astype(BF16), win_ref[...], preferred_element_type=F32)
    u = z[:, :d]
    g = z[:, d:]

    ubuf[off:off + rows, :] = u
    cw = cw_ref[...]
    uc = u * cw[CONV_W - 1:CONV_W, :] + cb_ref[...]
    for j in range(CONV_W - 1):
        lo = off - (CONV_W - 1 - j) * nb
        uc = uc + ubuf[lo:lo + rows, :] * cw[j:j + 1, :]
    ubuf[off - tail:off, :] = ubuf[off + rows - tail:off + rows, :]

    _rglru_gates(uc, wg_ref, bga_ref, bgx_ref, lam_ref, a_s, b_s)

    h = hcar[...]
    for s in range(tt):
        sl = slice(s * nb, (s + 1) * nb)
        h = a_s[sl, :] * h + b_s[sl, :]
        b_s[sl, :] = h
    hcar[...] = h
    hl_ref[0] = h
    cn_ref[0] = ubuf[off - tail:off, :]

    hg = b_s[...] * _silu(g)
    y = jnp.dot(hg.astype(BF16), wout_ref[...], preferred_element_type=F32)
    y_ref[0] = x + _rms(y, post_ref[...])


def _layer_a(x, h0, c0, pre, post, win, cw, cb, wg, bga, bgx, lam, wout, *, tt, nb):
    g_, rows_total, d = x.shape
    rows = tt * nb
    nt = rows_total // rows
    assert nt * rows == rows_total and tt >= CONV_W - 1
    assert (nb == 1 and tt % SUBLANES == 0) or nb % SUBLANES == 0
    tail = (CONV_W - 1) * nb
    off = -(-tail // SUBLANES) * SUBLANES
    row = lambda v: v.reshape(1, d)
    if nb == 1:
        assert tt % (SUBLANES * SUBLANES) == 0 and tt // SUBLANES > CONV_W - 1
        sps = SEQS_PER_STEP if g_ % SEQS_PER_STEP == 0 else 1
        kern = functools.partial(_layer_a_seq_kernel, tt=tt)
        scratch = [
            pltpu.VMEM((sps, tail * SUBLANES + rows, d), F32),
            pltpu.VMEM((sps, rows, d), F32),
            pltpu.VMEM((sps, rows, d), F32),
            pltpu.VMEM((sps, rows, d), F32),
            pltpu.VMEM((sps, 1, d), F32),
            pltpu.VMEM((sps, SUBLANES, d), F32),
            pltpu.VMEM((2, tt, tt), BF16),
        ]
    else:
        sps = 1
        kern = functools.partial(_layer_a_kernel, tt=tt, nb=nb, off=off)
        scratch = [
            pltpu.VMEM((off + rows, d), F32),
            pltpu.VMEM((rows, d), F32),
            pltpu.VMEM((rows, d), F32),
            pltpu.VMEM((nb, d), F32),
        ]
    return pl.pallas_call(
        kern,
        grid=(g_ // sps, nt),
        in_specs=[
            pl.BlockSpec((sps, rows, d), lambda g, t: (g, t, 0)),
            pl.BlockSpec((sps, nb, d), lambda g, t: (g, 0, 0)),
            pl.BlockSpec((sps, tail, d), lambda g, t: (g, 0, 0)),
            _const_spec((1, d)), _const_spec((1, d)),
            _const_spec(win.shape), _const_spec(cw.shape), _const_spec((1, d)),
            _const_spec(wg.shape), _const_spec((1, d)), _const_spec((1, d)), _const_spec((1, d)),
            _const_spec(wout.shape),
        ],
        out_specs=[
            pl.BlockSpec((sps, rows, d), lambda g, t: (g, t, 0)),
            pl.BlockSpec((sps, nb, d), lambda g, t: (g, 0, 0)),
            pl.BlockSpec((sps, tail, d), lambda g, t: (g, 0, 0)),
        ],
        out_shape=[
            jax.ShapeDtypeStruct((g_, rows_total, d), F32),
            jax.ShapeDtypeStruct((g_, nb, d), F32),
            jax.ShapeDtypeStruct((g_, tail, d), F32),
        ],
        scratch_shapes=scratch,
        compiler_params=_params("parallel", "arbitrary"),
        name="rglru_layer",
    )(x, h0, c0, row(pre), row(post), win, cw, row(cb), wg, row(bga), row(bgx), row(lam), wout)


def _aug_base(h, hd):
    upper = ((h * hd) % LANES) != 0
    return (0 if upper else hd), upper


def _head_block_t(xt, row, h, hd, key_side):
    cols = xt.shape[1]
    _, upper = _aug_base(h, hd)
    sub = lax.broadcasted_iota(jnp.int32, (SUBLANES, cols), 0)
    hi, mid, lo = _split3((-LOG2E if key_side else LOG2E) * row)
    first = N_AUG if key_side else 0
    ones = (sub < 2 * N_AUG) & ((sub >= N_AUG) != key_side)
    aug = jnp.where(sub == first, hi,
                    jnp.where(sub == first + 1, mid,
                              jnp.where(sub == first + 2, lo,
                                        jnp.where(ones, 1.0, 0.0))))
    pad = jnp.zeros((LANES - hd - SUBLANES, cols), F32)
    data = xt[h * hd:(h + 1) * hd, :]
    parts = [aug, pad, data] if upper else [data, aug, pad]
    return jnp.concatenate(parts, axis=0).astype(BF16)


def _kv_prompt_kernel(x_ref, g_ref, wkvt_ref, bfc_ref,
                      kt_ref, vt_ref, lft_ref, ct_ref, kp_ref, vtb_ref, carry_c, *, n_heads):
    x = x_ref[0]
    tt, d = x.shape
    hd = d // n_heads
    xn = _rms(x, g_ref[...]).astype(BF16)
    nt_dot = lambda lo, hi: lax.dot_general(wkvt_ref[lo:hi, :], xn, _NT,
                                            preferred_element_type=F32)
    kt = nt_dot(0, d)
    vt = nt_dot(d, 2 * d)
    lft = _log_sigmoid(nt_dot(2 * d, 2 * d + n_heads) + bfc_ref[...])
    kt_ref[0] = kt
    vt_ref[0] = vt
    vtb_ref[0] = vt.astype(BF16)
    lft_ref[0] = lft

    @pl.when(pl.program_id(1) == 0)
    def _():
        carry_c[...] = jnp.zeros_like(carry_c)

    r = lax.broadcasted_iota(jnp.int32, (tt, tt), 0)
    c = lax.broadcasted_iota(jnp.int32, (tt, tt), 1)
    cst = _exact_dot_rhs(lft, jnp.where(r <= c, 1.0, 0.0).astype(BF16)) + carry_c[...]
    carry_c[...] = cst[:, tt - 1:tt]
    ct_ref[0] = cst
    for h in range(n_heads):
        kp_ref[0, h] = _head_block_t(kt, cst[h:h + 1, :], h, hd, key_side=True).T


def _kv_prompt(x, g, wkvt, bfc, *, tt, n_heads):
    b, t, d = x.shape
    nt = t // tt
    assert nt * tt == t
    feat = lambda n, dt: (pl.BlockSpec((1, n, tt), lambda i, j: (i, 0, j)),
                          jax.ShapeDtypeStruct((b, n, t), dt))
    outs = [feat(d, F32), feat(d, F32), feat(n_heads, F32), feat(n_heads, F32),
            (pl.BlockSpec((1, n_heads, tt, LANES), lambda i, j: (i, 0, j, 0)),
             jax.ShapeDtypeStruct((b, n_heads, t, LANES), BF16)),
            feat(d, BF16)]
    return pl.pallas_call(
        functools.partial(_kv_prompt_kernel, n_heads=n_heads),
        grid=(b, nt),
        in_specs=[pl.BlockSpec((1, tt, d), lambda i, j: (i, j, 0)), _const_spec((1, d)),
                  _const_spec(wkvt.shape), _const_spec(bfc.shape)],
        out_specs=[o[0] for o in outs],
        out_shape=[o[1] for o in outs],
        scratch_shapes=[pltpu.VMEM((n_heads, 1), F32)],
        compiler_params=_params("parallel", "arbitrary"),
        name="kv_prompt",
    )(x, g.reshape(1, d), wkvt, bfc)


def _qg_prompt_kernel(x_ref, g_ref, wt_ref, ct_ref, qt_ref, gt_ref, *, n_heads):
    x = x_ref[0]
    d = x.shape[-1]
    hd = d // n_heads
    xn = _rms(x, g_ref[...]).astype(BF16)
    zt = lax.dot_general(wt_ref[...], xn, _NT, preferred_element_type=F32)
    gt_ref[0] = zt[d:, :].astype(BF16)
    qt = zt[:d, :] * (hd ** -0.5 * LOG2E)
    ct = ct_ref[0]
    for h in range(n_heads):
        qt_ref[0, h] = _head_block_t(qt, ct[h:h + 1, :], h, hd, key_side=False)


def _qg_prompt(x, g, wt, ct, *, tt, n_heads):
    b, t, d = x.shape
    nt = t // tt
    assert nt * tt == t
    return pl.pallas_call(
        functools.partial(_qg_prompt_kernel, n_heads=n_heads),
        grid=(b, nt),
        in_specs=[pl.BlockSpec((1, tt, d), lambda i, j: (i, j, 0)), _const_spec((1, d)),
                  _const_spec(wt.shape),
                  pl.BlockSpec((1, n_heads, tt), lambda i, j: (i, 0, j))],
        out_specs=[pl.BlockSpec((1, n_heads, LANES, tt), lambda i, j: (i, 0, 0, j)),
                   pl.BlockSpec((1, d, tt), lambda i, j: (i, 0, j))],
        out_shape=[jax.ShapeDtypeStruct((b, n_heads, LANES, t), BF16),
                   jax.ShapeDtypeStruct((b, d, t), BF16)],
        compiler_params=_params("parallel", "parallel"),
        name="qg_prompt",
    )(x, g.reshape(1, d), wt, ct)


ATTN_FULL2, ATTN_FULL_DIAG, ATTN_DIAG = 0, 1, 2


def _attn_kernel(qi_tab, kj_tab, kind_tab, q_ref, k_ref, v_ref, o_ref, m_s, l_s, al_s, mc_s, acc_s,
                 *bufs, hd):
    st_bufs, pt_bufs = bufs[:4], bufs[4:]
    step = pl.program_id(1)
    kj = kj_tab[step]
    kind = kind_tab[step]
    n_heads, _, tq = q_ref.shape[1:]
    tk = tq
    n_pairs = n_heads // 2

    @pl.when(kj == 0)
    def _():
        m_s[...] = jnp.full(m_s.shape, -jnp.inf, F32)
        l_s[...] = jnp.zeros_like(l_s)
        acc_s[...] = jnp.zeros_like(acc_s)

    def run(diag_flags):
        items = [(kb, pair) for kb in range(len(diag_flags)) for pair in range(n_pairs)]

        def scores(n, s):
            kb, pair = items[n]
            for j in range(2):
                h = 2 * pair + j
                st = jnp.dot(k_ref[0, h, kb * tk:(kb + 1) * tk, :], q_ref[0, h],
                             preferred_element_type=F32)
                if diag_flags[kb]:
                    keep = (lax.broadcasted_iota(jnp.int32, (tk, tq), 0)
                            <= lax.broadcasted_iota(jnp.int32, (tk, tq), 1))
                    st = jnp.where(keep, st, -jnp.inf)
                st_bufs[2 * s + j][...] = st
                mc_s[2 * s + j] = jnp.max(st, axis=0, keepdims=True)

        def softmax(n, s):
            for j in range(2):
                h = 2 * items[n][1] + j
                m_prev = m_s[h]
                m_new = jnp.maximum(m_prev, mc_s[2 * s + j])
                m_s[h] = m_new
                al_s[h] = jnp.exp2(m_prev - m_new)
                pt_bufs[2 * s + j][...] = jnp.exp2(st_bufs[2 * s + j][...] - m_new).astype(BF16)

        ones_rows = jnp.ones((BF16_ROWS, tk), BF16)

        def values(n, s):
            kb, pair = items[n]
            for j in range(2):
                h = 2 * pair + j
                rows = slice(h * hd, (h + 1) * hd)
                pv = jnp.dot(jnp.concatenate([v_ref[0, rows, kb * tk:(kb + 1) * tk], ones_rows],
                                             axis=0),
                             pt_bufs[2 * s + j][...], preferred_element_type=F32)
                alpha = al_s[h]
                acc_s[rows, :] = acc_s[rows, :] * alpha + pv[:hd, :]
                l_s[h] = alpha * l_s[h] + pv[hd:hd + 1, :]

        for t in range(len(items) + 2):
            if t < len(items):
                scores(t, t % 2)
            if 1 <= t <= len(items):
                softmax(t - 1, (t - 1) % 2)
            if t >= 2:
                values(t - 2, t % 2)

    def finish():
        for h in range(n_heads):
            rows = slice(h * hd, (h + 1) * hd)
            o_ref[0, rows, :] = (acc_s[rows, :] * (1.0 / l_s[h])).astype(o_ref.dtype)

    @pl.when(kind == ATTN_FULL2)
    def _():
        run((False, False))

    @pl.when(kind == ATTN_FULL_DIAG)
    def _():
        run((False, True))
        finish()

    @pl.when(kind == ATTN_DIAG)
    def _():
        run((True,))
        finish()


def _prompt_attention(qt, kp, vtb, *, blk, hd):
    b, n_heads, _, t = qt.shape
    d = vtb.shape[1]
    nq = t // blk
    assert nq * blk == t and nq % 2 == 0
    qi_l, kj_l, kind_l = [], [], []
    for i in range(nq):
        for j in range(i // 2 + 1):
            qi_l.append(i)
            kj_l.append(j)
            kind_l.append(ATTN_FULL2 if 2 * j + 1 < i else
                          ATTN_FULL_DIAG if 2 * j + 1 == i else ATTN_DIAG)
    grid_spec = pltpu.PrefetchScalarGridSpec(
        num_scalar_prefetch=3,
        grid=(b, len(qi_l)),
        in_specs=[
            pl.BlockSpec((1, n_heads, LANES, blk), lambda i, s, qt_, kt_, kd_: (i, 0, 0, qt_[s])),
            pl.BlockSpec((1, n_heads, 2 * blk, LANES),
                         lambda i, s, qt_, kt_, kd_: (i, 0, kt_[s], 0)),
            pl.BlockSpec((1, d, 2 * blk), lambda i, s, qt_, kt_, kd_: (i, 0, kt_[s])),
        ],
        out_specs=pl.BlockSpec((1, d, blk), lambda i, s, qt_, kt_, kd_: (i, 0, qt_[s])),
        scratch_shapes=[
            pltpu.VMEM((n_heads, 1, blk), F32),
            pltpu.VMEM((n_heads, 1, blk), F32),
            pltpu.VMEM((n_heads, 1, blk), F32),
            pltpu.VMEM((4, 1, blk), F32),
            pltpu.VMEM((d, blk), F32),
        ] + [pltpu.VMEM((blk, blk), F32)] * 4 + [pltpu.VMEM((blk, blk), BF16)] * 4,
    )
    return pl.pallas_call(
        functools.partial(_attn_kernel, hd=hd),
        grid_spec=grid_spec,
        out_shape=jax.ShapeDtypeStruct((b, d, t), BF16),
        compiler_params=_params("parallel", "arbitrary"),
        name="prompt_attention",
    )(jnp.asarray(qi_l, jnp.int32), jnp.asarray(kj_l, jnp.int32), jnp.asarray(kind_l, jnp.int32),
      qt, kp, vtb)


def _out_prompt_kernel(ot_ref, gt_ref, x_ref, w_ref, post_ref, y_ref):
    hgt = ot_ref[0].astype(F32) * _silu(gt_ref[0].astype(F32))
    y = lax.dot_general(hgt.astype(BF16), w_ref[...], _TN, preferred_element_type=F32)
    y_ref[0] = x_ref[0] + _rms(y, post_ref[...])


def _out_prompt(ot, gt, x, wt, post, *, tt):
    b, t, d = x.shape
    nt = t // tt
    assert nt * tt == t
    row_spec = pl.BlockSpec((1, tt, d), lambda i, j: (i, j, 0))
    feat_spec = pl.BlockSpec((1, d, tt), lambda i, j: (i, 0, j))
    return pl.pallas_call(
        _out_prompt_kernel,
        grid=(b, nt),
        in_specs=[feat_spec, feat_spec, row_spec, _const_spec(wt.shape), _const_spec((1, d))],
        out_specs=row_spec,
        out_shape=jax.ShapeDtypeStruct((b, t, d), F32),
        compiler_params=_params("parallel", "parallel"),
        name="out_prompt",
    )(ot, gt, x, wt, post.reshape(1, d))


def _kv_sample_kernel(x_ref, g_ref, wk_ref, wv_ref, wf_ref, bfr_ref, k_ref, v_ref, lf_ref,
                      *, n_heads):
    xn = _rms(x_ref[...], g_ref[...]).astype(BF16)
    k_ref[...] = jnp.dot(xn, wk_ref[...], preferred_element_type=F32)
    v_ref[...] = jnp.dot(xn, wv_ref[...], preferred_element_type=F32)
    lf = _log_sigmoid(jnp.dot(xn, wf_ref[...], preferred_element_type=F32) + bfr_ref[...])
    lf_ref[...] = lf[:, :n_heads]


def _kv_sample(x, g, wk, wv, wf, bfr, *, n_heads):
    rows, d = x.shape
    return pl.pallas_call(
        functools.partial(_kv_sample_kernel, n_heads=n_heads),
        out_shape=[jax.ShapeDtypeStruct((rows, d), F32), jax.ShapeDtypeStruct((rows, d), F32),
                   jax.ShapeDtypeStruct((rows, n_heads), F32)],
        compiler_params=pltpu.CompilerParams(vmem_limit_bytes=VMEM_LIMIT),
        name="kv_sample",
    )(x, g.reshape(1, d), wk, wv, wf, bfr)


def _qg_sample_kernel(x_ref, g_ref, wt_ref, q_ref, gate_ref, *, n_heads):
    d = x_ref.shape[-1]
    xn = _rms(x_ref[...], g_ref[...]).astype(BF16)
    z = lax.dot_general(xn, wt_ref[...], _NT, preferred_element_type=F32)
    q_ref[...] = z[:, :d] * ((d // n_heads) ** -0.5)
    gate_ref[...] = z[:, d:]


def _qg_sample(x, g, w, *, n_heads):
    rows, d = x.shape
    return pl.pallas_call(
        functools.partial(_qg_sample_kernel, n_heads=n_heads),
        out_shape=[jax.ShapeDtypeStruct((rows, d), F32), jax.ShapeDtypeStruct((rows, d), F32)],
        compiler_params=pltpu.CompilerParams(vmem_limit_bytes=VMEM_LIMIT),
        name="qg_sample",
    )(x, g.reshape(1, d), w)


def _out_sample_kernel(o_ref, gate_ref, x_ref, w_ref, post_ref, y_ref):
    hg = o_ref[...] * _silu(gate_ref[...])
    y = jnp.dot(hg.astype(BF16), w_ref[...], preferred_element_type=F32)
    y_ref[...] = x_ref[...] + _rms(y, post_ref[...])


def _out_sample(o, gate, x, w, post):
    rows, d = x.shape
    return pl.pallas_call(
        _out_sample_kernel,
        out_shape=jax.ShapeDtypeStruct((rows, d), F32),
        compiler_params=pltpu.CompilerParams(vmem_limit_bytes=VMEM_LIMIT),
        name="out_sample",
    )(o, gate, x, w, post.reshape(1, d))


PAGE_GROUP = 4
RING_GROUPS = 8
N_SLOTS = RING_GROUPS * PAGE_GROUP


def _decode_kernel(pt_ref, q_ref, kn_ref, vn_ref, lfn_ref, kc_hbm, vc_hbm, lfc_hbm, o_ref,
                   kvbuf, lbuf, sem_kv, sem_l, s_ref, p_ref, acc_ref, qbd_ref, *, n_heads):
    seq = pl.program_id(0)
    n_seq = pl.num_programs(0)
    n_pages = pt_ref.shape[1]
    d, page = kvbuf.shape[1:]
    nq = q_ref.shape[1]
    hd = d // n_heads
    rows = n_heads * nq
    assert rows == LANES and page == LANES and n_pages % PAGE_GROUP == 0
    k_groups = n_pages // PAGE_GROUP
    n_groups = 2 * k_groups

    def kv_copy(src_hbm, page_id, slot):
        return pltpu.make_async_copy(src_hbm.at[page_id], kvbuf.at[slot], sem_kv.at[slot])

    def lf_copy(page_id, slot):
        return pltpu.make_async_copy(lfc_hbm.at[page_id], lbuf.at[slot], sem_l.at[slot])

    first = seq * n_groups

    def group_base(gid):
        return lax.rem(gid, RING_GROUPS) * PAGE_GROUP

    def start_group(gid):
        base = group_base(gid)
        sq = lax.div(gid, n_groups)
        grp = lax.rem(gid, n_groups)
        live = gid < n_seq * n_groups

        @pl.when(live & (grp < k_groups))
        def _():
            for j in range(PAGE_GROUP):
                pid = pt_ref[sq, grp * PAGE_GROUP + j]
                kv_copy(kc_hbm, pid, base + j).start()
                lf_copy(pid, base + j).start()

        @pl.when(live & (grp >= k_groups))
        def _():
            for j in range(PAGE_GROUP):
                pid = pt_ref[sq, (grp - k_groups) * PAGE_GROUP + j]
                kv_copy(vc_hbm, pid, base + j).start()

    @pl.when(seq == 0)
    def _():
        for g in range(RING_GROUPS):
            start_group(jnp.int32(g))

    q = q_ref[0]
    lane_head = lax.broadcasted_iota(jnp.int32, (nq, d), 1) // hd
    qbd_ref[...] = jnp.concatenate([jnp.where(lane_head == h, q, 0.0) for h in range(n_heads)],
                                   axis=0).astype(BF16)

    r = lax.broadcasted_iota(jnp.int32, (page, 2 * page), 0)
    c = lax.broadcasted_iota(jnp.int32, (page, 2 * page), 1)
    tri_ones = jnp.where((r <= c) | (c >= page), 1.0, 0.0).astype(BF16)

    def head_rows(ct):
        return jnp.concatenate(
            [jnp.broadcast_to(ct[h:h + 1, :], (nq, ct.shape[1])) for h in range(n_heads)], axis=0)

    def k_group(g, carry):
        gid = first + g
        base = group_base(gid)
        for j in range(PAGE_GROUP):
            kv_copy(kc_hbm, 0, base + j).wait()
            lf_copy(0, base + j).wait()
        for j in range(PAGE_GROUP):
            sc = jnp.dot(qbd_ref[...], kvbuf[base + j].astype(BF16), preferred_element_type=F32)
            cum = _exact_dot_rhs(lbuf[base + j], tri_ones)
            s_ref[g * PAGE_GROUP + j] = sc - head_rows(carry + cum[:, :page])
            carry = carry + cum[:, page:]
        start_group(gid + RING_GROUPS)
        return carry

    past_total = lax.fori_loop(0, k_groups, k_group, jnp.zeros((n_heads, page), F32))

    ct_new = past_total + _exact_dot_rhs(lfn_ref[0], tri_ones)[:, :page]
    kn = jnp.concatenate([kn_ref[0], jnp.zeros((page - nq, d), F32)], axis=0).astype(BF16)
    sc_new = lax.dot_general(qbd_ref[...], kn, _NT, preferred_element_type=F32)
    key_idx = lax.broadcasted_iota(jnp.int32, (rows, page), 1)
    q_idx = lax.rem(lax.broadcasted_iota(jnp.int32, (rows, page), 0), nq)
    ct_rows = head_rows(ct_new)
    cq = jnp.sum(jnp.where(key_idx == q_idx, ct_rows, 0.0), axis=1, keepdims=True)
    s_ref[n_pages] = jnp.where(key_idx <= q_idx, sc_new - ct_rows, -jnp.inf)

    def max_body(i, m):
        return jnp.maximum(m, s_ref[i])
    m_el = lax.fori_loop(0, n_pages + 1, max_body, jnp.full((rows, page), -jnp.inf, F32))
    m = jnp.max(m_el, axis=1, keepdims=True) + cq

    def p_body(i, l):
        pe = jnp.exp((s_ref[i] + cq) - m)
        p_ref[i] = pe.astype(BF16)
        return l + pe
    l_el = lax.fori_loop(0, n_pages + 1, p_body, jnp.zeros((rows, page), F32))
    l = jnp.sum(l_el, axis=1, keepdims=True)

    vn = jnp.concatenate([vn_ref[0], jnp.zeros((page - nq, d), F32)], axis=0).astype(BF16)
    acc_ref[...] = jnp.dot(p_ref[n_pages], vn, preferred_element_type=F32)

    def v_group(g, carry):
        gid = first + k_groups + g
        base = group_base(gid)
        for j in range(PAGE_GROUP):
            kv_copy(vc_hbm, 0, base + j).wait()
        upd = None
        for j in range(PAGE_GROUP):
            pv = lax.dot_general(p_ref[g * PAGE_GROUP + j], kvbuf[base + j].astype(BF16), _NT,
                                 preferred_element_type=F32)
            upd = pv if upd is None else upd + pv
        acc_ref[...] += upd
        start_group(gid + RING_GROUPS)
        return carry

    lax.fori_loop(0, k_groups, v_group, 0)

    acc = acc_ref[...] * (1.0 / l)
    out = jnp.zeros((nq, d), F32)
    for h in range(n_heads):
        out = out + jnp.where(lane_head == h, acc[h * nq:(h + 1) * nq, :], 0.0)
    o_ref[0] = out


def _decode_attention(page_table, q, k_new, v_new, lft_new, cache_kt, cache_vt, cache_lft, *, n_heads):
    b, nq, d = q.shape
    n_pages = page_table.shape[1]
    page = cache_kt.shape[2]
    rows = n_heads * nq
    seq_spec = pl.BlockSpec((1, nq, d), lambda i, pt: (i, 0, 0))
    grid_spec = pltpu.PrefetchScalarGridSpec(
        num_scalar_prefetch=1,
        grid=(b,),
        in_specs=[
            seq_spec, seq_spec, seq_spec,
            pl.BlockSpec((1, n_heads, page), lambda i, pt: (i, 0, 0)),
            pl.BlockSpec(memory_space=pl.ANY),
            pl.BlockSpec(memory_space=pl.ANY),
            pl.BlockSpec(memory_space=pl.ANY),
        ],
        out_specs=seq_spec,
        scratch_shapes=[
            pltpu.VMEM((N_SLOTS, d, page), F32),
            pltpu.VMEM((N_SLOTS, n_heads, page), F32),
            pltpu.SemaphoreType.DMA((N_SLOTS,)),
            pltpu.SemaphoreType.DMA((N_SLOTS,)),
            pltpu.VMEM((n_pages + 1, rows, page), F32),
            pltpu.VMEM((n_pages + 1, rows, page), BF16),
            pltpu.VMEM((rows, d), F32),
            pltpu.VMEM((rows, d), BF16),
        ],
    )
    return pl.pallas_call(
        functools.partial(_decode_kernel, n_heads=n_heads),
        grid_spec=grid_spec,
        out_shape=jax.ShapeDtypeStruct((b, nq, d), F32),
        compiler_params=_params("arbitrary"),
        name="decode_attention",
    )(page_table, q, k_new, v_new, lft_new, cache_kt, cache_vt, cache_lft)


def kernel(x_prompt, x_sample, cache_k, cache_v, cache_logf, state_rnn, state_conv, page_table,
           a_pre_norm, a_post_norm, a_w_in, a_conv_w, a_conv_b, a_w_ga, a_b_ga, a_w_gx, a_b_gx,
           a_lambda, a_w_out, kv_norm, w_kv, b_f, b_pre_norm, b_post_norm, b_w_in, b_w_out):
    n_a = a_w_in.shape[0]
    n_b = b_w_in.shape[0]
    d = x_prompt.shape[-1]
    n_heads = b_f.shape[0]
    hd = d // n_heads
    assert 2 * hd == LANES and n_heads <= LANES

    a_win = a_w_in.astype(BF16)
    a_wout = a_w_out.astype(BF16)
    a_wg = jnp.concatenate([a_w_ga, a_w_gx], axis=-1).astype(BF16)
    w_kv_t = w_kv.T.astype(BF16)
    wk = w_kv[:, :d].astype(BF16)
    wv = w_kv[:, d:2 * d].astype(BF16)
    wf = jnp.pad(w_kv[:, 2 * d:], ((0, 0), (0, LANES - n_heads))).astype(BF16)
    bfr = jnp.pad(b_f, (0, LANES - n_heads)).reshape(1, LANES)
    bfc = b_f.reshape(n_heads, 1)
    b_wout = b_w_out.astype(BF16)
    b_win_t = jnp.swapaxes(b_w_in, 1, 2).astype(BF16)

    def a_layer(l, x, h0, c0, tt, nb):
        return _layer_a(x, h0, c0, a_pre_norm[l], a_post_norm[l], a_win[l], a_conv_w[l], a_conv_b[l],
                        a_wg[l], a_b_ga[l], a_b_gx[l], a_lambda[l], a_wout[l], tt=tt, nb=nb)

    bp, tp, _ = x_prompt.shape
    x = x_prompt
    rnn_p, conv_p = [], []
    for l in range(n_a):
        x, hl, cn = a_layer(l, x, jnp.zeros((bp, 1, d), F32), jnp.zeros((bp, CONV_W - 1, d), F32),
                            tt=_tile(tp, RGLRU_TILE), nb=1)
        rnn_p.append(hl.reshape(bp, d))
        conv_p.append(cn)
    tt = _tile(tp, PROMPT_TILE)
    kt_p, vt_p, lft_p, ct_p, kaug_p, vtb_p = _kv_prompt(x, kv_norm, w_kv_t, bfc, tt=tt,
                                                        n_heads=n_heads)
    for l in range(n_b):
        qt, gt = _qg_prompt(x, b_pre_norm[l], b_win_t[l], ct_p, tt=tt, n_heads=n_heads)
        ot = _prompt_attention(qt, kaug_p, vtb_p, blk=tt, hd=hd)
        x = _out_prompt(ot, gt, x, b_wout[l], b_post_norm[l], tt=tt)
    y_prompt = x
    k_p = jnp.transpose(kt_p.reshape(bp, n_heads, hd, tp), (0, 3, 1, 2))
    v_p = jnp.transpose(vt_p.reshape(bp, n_heads, hd, tp), (0, 3, 1, 2))
    lf_p = jnp.swapaxes(lft_p, 1, 2)

    bs, ts, _ = x_sample.shape
    rows_s = bs * ts
    n_pool, page = cache_k.shape[:2]
    x = jnp.swapaxes(x_sample, 0, 1).reshape(1, rows_s, d)
    rnn_s, conv_s = [], []
    for l in range(n_a):
        c0 = jnp.swapaxes(state_conv[l], 0, 1).reshape(1, (CONV_W - 1) * bs, d)
        x, hl, cn = a_layer(l, x, state_rnn[l].reshape(1, bs, d), c0, tt=ts, nb=bs)
        rnn_s.append(hl.reshape(bs, d))
        conv_s.append(jnp.swapaxes(cn.reshape(CONV_W - 1, bs, d), 0, 1))
    x = jnp.swapaxes(x.reshape(ts, bs, d), 0, 1).reshape(rows_s, d)
    k_s, v_s, lf_s = _kv_sample(x, kv_norm, wk, wv, wf, bfr, n_heads=n_heads)
    lft_new = jnp.pad(jnp.swapaxes(lf_s.reshape(bs, ts, n_heads), 1, 2),
                      ((0, 0), (0, 0), (0, page - ts)))
    cache_kt = jnp.transpose(cache_k, (0, 2, 3, 1)).reshape(n_pool, d, page)
    cache_vt = jnp.transpose(cache_v, (0, 2, 3, 1)).reshape(n_pool, d, page)
    cache_lft = jnp.swapaxes(cache_logf, 1, 2)
    for l in range(n_b):
        q, gate = _qg_sample(x, b_pre_norm[l], b_win_t[l], n_heads=n_heads)
        o = _decode_attention(page_table, q.reshape(bs, ts, d), k_s.reshape(bs, ts, d),
                              v_s.reshape(bs, ts, d), lft_new, cache_kt, cache_vt, cache_lft,
                              n_heads=n_heads)
        x = _out_sample(o.reshape(rows_s, d), gate, x, b_wout[l], b_post_norm[l])
    y_sample = x.reshape(bs, ts, d)

    return (y_prompt, y_sample, k_p, v_p, lf_p,
            jnp.stack(rnn_p), jnp.stack(conv_p),
            k_s.reshape(bs, ts, n_heads, hd), v_s.reshape(bs, ts, n_heads, hd),
            lf_s.reshape(bs, ts, n_heads),
            jnp.stack(rnn_s), jnp.stack(conv_s))
```

```python
import functools

import jax
import jax.numpy as jnp
from jax import lax
from jax.experimental import pallas as pl
from jax.experimental.pallas import tpu as pltpu

F32 = jnp.float32
BF16 = jnp.bfloat16

EPS = 1e-6
RGLRU_C = 8.0
CONV_W = 4
LANES = 128
SUBLANES = 8
BF16_ROWS = 16
N_AUG = 3
LOG2E = 1.4426950408889634
VMEM_LIMIT = 56 * 1024 * 1024
RGLRU_TILE = 256
PROMPT_TILE = 512


def _tile(n, want):
    t = min(n, want)
    while n % t or t % LANES:
        t -= LANES
    return t

_NT = (((1,), (1,)), ((), ()))
_TN = (((0,), (0,)), ((), ()))


def _rms(x, g):
    ms = jnp.mean(x * x, axis=-1, keepdims=True)
    return x * lax.rsqrt(ms + EPS) * g


def _softplus(x):
    return jnp.maximum(x, 0.0) + jnp.log1p(jnp.exp(-jnp.abs(x)))


def _log_sigmoid(x):
    return jnp.minimum(x, 0.0) - jnp.log1p(jnp.exp(-jnp.abs(x)))


def _expm1_of_square(x, u):
    d = u - 1.0
    edge = (d == 0.0) | (d == -1.0)
    r = d * x / jnp.log(jnp.where(edge, 0.5, u))
    return jnp.where(d == 0.0, x, jnp.where(d == -1.0, -1.0, r))


def _silu(x):
    return x * jax.nn.sigmoid(x)


def _split3(x):
    hi = x.astype(BF16).astype(F32)
    r1 = x - hi
    mid = r1.astype(BF16).astype(F32)
    lo = (r1 - mid).astype(BF16).astype(F32)
    return hi, mid, lo


def _exact_dot(ones_lhs, x):
    hi, mid, lo = _split3(x)
    d = lambda p: jnp.dot(ones_lhs, p.astype(BF16), preferred_element_type=F32)
    return d(hi) + d(mid) + d(lo)


def _exact_dot_rhs(x, ones_rhs):
    hi, mid, lo = _split3(x)
    d = lambda p: jnp.dot(p.astype(BF16), ones_rhs, preferred_element_type=F32)
    return d(hi) + d(mid) + d(lo)


def _const_spec(shape):
    nd = len(shape)
    return pl.BlockSpec(shape, lambda *_: (0,) * nd)


def _params(*sem):
    return pltpu.CompilerParams(dimension_semantics=sem, vmem_limit_bytes=VMEM_LIMIT)


SEQS_PER_STEP = 2


def _rglru_gates(uc, wg_ref, bga_ref, bgx_ref, lam_ref, a_s, b_s):
    d = uc.shape[-1]
    nblk = wg_ref.shape[0]
    blk = d // nblk
    rate = (-RGLRU_C) * _softplus(-lam_ref[...])
    ucb = uc.astype(BF16)
    for n in range(nblk):
        sl = slice(n * blk, (n + 1) * blk)
        gz = jnp.dot(ucb[:, sl], wg_ref[n], preferred_element_type=F32)
        r = jax.nn.sigmoid(gz[:, :blk] + bga_ref[:, sl])
        i = jax.nn.sigmoid(gz[:, blk:] + bgx_ref[:, sl])
        log_a = r * rate[:, sl]
        a = jnp.exp(log_a)
        a_s[:, sl] = a
        b_s[:, sl] = jnp.sqrt(-_expm1_of_square(2.0 * log_a, a * a)) * (i * uc[:, sl])


def _layer_a_seq_kernel(x_ref, h0_ref, c0_ref, pre_ref, post_ref, win_ref, cw_ref, cb_ref,
                        wg_ref, bga_ref, bgx_ref, lam_ref, wout_ref,
                        y_ref, hl_ref, cn_ref,
                        ubuf, a_s, b_s, g_s, hcar, tail_s, perm_s, *, tt):
    t = pl.program_id(1)
    n_seq, _, d = x_ref.shape
    seg = tt // SUBLANES
    n_tail = CONV_W - 1
    head = n_tail * SUBLANES

    @pl.when(t == 0)
    def _():
        hcar[...] = h0_ref[...]
        tail_s[:, 0:n_tail, :] = c0_ref[...]
        r = lax.broadcasted_iota(jnp.int32, (tt, tt), 0)
        c = lax.broadcasted_iota(jnp.int32, (tt, tt), 1)
        time_of = lambda i: (i & (SUBLANES - 1)) * seg + (i >> 3)
        perm_s[0] = jnp.where(c == time_of(r), 1.0, 0.0).astype(BF16)
        perm_s[1] = jnp.where(r == time_of(c), 1.0, 0.0).astype(BF16)

    for b in range(n_seq):
        xn = _rms(x_ref[b], pre_ref[...]).astype(BF16)
        xp = jnp.dot(perm_s[0], xn, preferred_element_type=F32).astype(BF16)
        z = jnp.dot(xp, win_ref[...], preferred_element_type=F32)
        u = z[:, :d]
        g_s[b] = z[:, d:]

        ubuf[b, head:head + tt, :] = u
        sub = lax.broadcasted_iota(jnp.int32, (SUBLANES, d), 0)
        for j in range(n_tail):
            grp = u[(seg - n_tail + j) * SUBLANES:(seg - n_tail + j + 1) * SUBLANES, :]
            ubuf[b, j * SUBLANES:(j + 1) * SUBLANES, :] = jnp.where(
                sub == 0, tail_s[b, j:j + 1, :], pltpu.roll(grp, 1, 0))
            tail_s[b, j:j + 1, :] = grp[SUBLANES - 1:SUBLANES, :]
        cn_ref[b] = tail_s[b, 0:n_tail, :]

        cw = cw_ref[...]
        uc = u * cw[n_tail:CONV_W, :] + cb_ref[...]
        for j in range(n_tail):
            uc = uc + ubuf[b, j * SUBLANES:j * SUBLANES + tt, :] * cw[j:j + 1, :]

        _rglru_gates(uc, wg_ref, bga_ref, bgx_ref, lam_ref, a_s.at[b], b_s.at[b])

    def body(k, carry):
        rows = pl.ds(pl.multiple_of(k * SUBLANES, SUBLANES), SUBLANES)
        out = []
        for b in range(n_seq):
            h, prod = carry[b]
            a = a_s[b, rows, :]
            h = a * h + b_s[b, rows, :]
            prod = prod * a
            b_s[b, rows, :] = h
            a_s[b, rows, :] = prod
            out.append((h, prod))
        return tuple(out)
    init = (jnp.zeros((SUBLANES, d), F32), jnp.ones((SUBLANES, d), F32))
    ends = lax.fori_loop(0, seg, body, (init,) * n_seq)

    for b in range(n_seq):
        h_end, a_end = ends[b]
        c = hcar[b]
        enter = []
        for s in range(SUBLANES):
            enter.append(c)
            c = h_end[s:s + 1, :] + a_end[s:s + 1, :] * c
        hcar[b] = c
        hl_ref[b] = c
        enter = jnp.concatenate(enter, axis=0)[None]

        h = (b_s[b].reshape(seg, SUBLANES, d) + a_s[b].reshape(seg, SUBLANES, d) * enter)
        hg = (h.reshape(tt, d) * _silu(g_s[b])).astype(BF16)
        hg = jnp.dot(perm_s[1], hg, preferred_element_type=F32).astype(BF16)
        y = jnp.dot(hg, wout_ref[...], preferred_element_type=F32)
        y_ref[b] = x_ref[b] + _rms(y, post_ref[...])


def _layer_a_kernel(x_ref, h0_ref, c0_ref, pre_ref, post_ref, win_ref, cw_ref, cb_ref,
                    wg_ref, bga_ref, bgx_ref, lam_ref, wout_ref,
                    y_ref, hl_ref, cn_ref,
                    ubuf, a_s, b_s, hcar, *, tt, nb, off):
    t = pl.program_id(1)
    rows = tt * nb
    d = x_ref.shape[-1]
    tail = (CONV_W - 1) * nb

    @pl.when(t == 0)
    def _():
        hcar[...] = h0_ref[0]
        ubuf[off - tail:off, :] = c0_ref[0]

    x = x_ref[0]
    xn = _rms(x, pre_ref[...])
    z = jnp.dot(xn.astype(BF16), win_ref[...], preferred_element_type=F32)
    u = z[:, :d]
    g = z[:, d:]

    ubuf[off:off + rows, :] = u
    cw = cw_ref[...]
    uc = u * cw[CONV_W - 1:CONV_W, :] + cb_ref[...]
    for j in range(CONV_W - 1):
        lo = off - (CONV_W - 1 - j) * nb
        uc = uc + ubuf[lo:lo + rows, :] * cw[j:j + 1, :]
    ubuf[off - tail:off, :] = ubuf[off + rows - tail:off + rows, :]

    _rglru_gates(uc, wg_ref, bga_ref, bgx_ref, lam_ref, a_s, b_s)

    h = hcar[...]
    for s in range(tt):
        sl = slice(s * nb, (s + 1) * nb)
        h = a_s[sl, :] * h + b_s[sl, :]
        b_s[sl, :] = h
    hcar[...] = h
    hl_ref[0] = h
    cn_ref[0] = ubuf[off - tail:off, :]

    hg = b_s[...] * _silu(g)
    y = jnp.dot(hg.astype(BF16), wout_ref[...], preferred_element_type=F32)
    y_ref[0] = x + _rms(y, post_ref[...])


def _layer_a(x, h0, c0, pre, post, win, cw, cb, wg, bga, bgx, lam, wout, *, tt, nb):
    g_, rows_total, d = x.shape
    rows = tt * nb
    nt = rows_total // rows
    assert nt * rows == rows_total and tt >= CONV_W - 1
    assert (nb == 1 and tt % SUBLANES == 0) or nb % SUBLANES == 0
    tail = (CONV_W - 1) * nb
    off = -(-tail // SUBLANES) * SUBLANES
    row = lambda v: v.reshape(1, d)
    if nb == 1:
        assert tt % (SUBLANES * SUBLANES) == 0 and tt // SUBLANES > CONV_W - 1
        sps = SEQS_PER_STEP if g_ % SEQS_PER_STEP == 0 else 1
        kern = functools.partial(_layer_a_seq_kernel, tt=tt)
        scratch = [
            pltpu.VMEM((sps, tail * SUBLANES + rows, d), F32),
            pltpu.VMEM((sps, rows, d), F32),
            pltpu.VMEM((sps, rows, d), F32),
            pltpu.VMEM((sps, rows, d), F32),
            pltpu.VMEM((sps, 1, d), F32),
            pltpu.VMEM((sps, SUBLANES, d), F32),
            pltpu.VMEM((2, tt, tt), BF16),
        ]
    else:
        sps = 1
        kern = functools.partial(_layer_a_kernel, tt=tt, nb=nb, off=off)
        scratch = [
            pltpu.VMEM((off + rows, d), F32),
            pltpu.VMEM((rows, d), F32),
            pltpu.VMEM((rows, d), F32),
            pltpu.VMEM((nb, d), F32),
        ]
    return pl.pallas_call(
        kern,
        grid=(g_ // sps, nt),
        in_specs=[
            pl.BlockSpec((sps, rows, d), lambda g, t: (g, t, 0)),
            pl.BlockSpec((sps, nb, d), lambda g, t: (g, 0, 0)),
            pl.BlockSpec((sps, tail, d), lambda g, t: (g, 0, 0)),
            _const_spec((1, d)), _const_spec((1, d)),
            _const_spec(win.shape), _const_spec(cw.shape), _const_spec((1, d)),
            _const_spec(wg.shape), _const_spec((1, d)), _const_spec((1, d)), _const_spec((1, d)),
            _const_spec(wout.shape),
        ],
        out_specs=[
            pl.BlockSpec((sps, rows, d), lambda g, t: (g, t, 0)),
            pl.BlockSpec((sps, nb, d), lambda g, t: (g, 0, 0)),
            pl.BlockSpec((sps, tail, d), lambda g, t: (g, 0, 0)),
        ],
        out_shape=[
            jax.ShapeDtypeStruct((g_, rows_total, d), F32),
            jax.ShapeDtypeStruct((g_, nb, d), F32),
            jax.ShapeDtypeStruct((g_, tail, d), F32),
        ],
        scratch_shapes=scratch,
        compiler_params=_params("parallel", "arbitrary"),
        name="rglru_layer",
    )(x, h0, c0, row(pre), row(post), win, cw, row(cb), wg, row(bga), row(bgx), row(lam), wout)


def _aug_base(h, hd):
    upper = ((h * hd) % LANES) != 0
    return (0 if upper else hd), upper


def _head_block_t(xt, row, h, hd, key_side):
    cols = xt.shape[1]
    _, upper = _aug_base(h, hd)
    sub = lax.broadcasted_iota(jnp.int32, (SUBLANES, cols), 0)
    hi, mid, lo = _split3((-LOG2E if key_side else LOG2E) * row)
    first = N_AUG if key_side else 0
    ones = (sub < 2 * N_AUG) & ((sub >= N_AUG) != key_side)
    aug = jnp.where(sub == first, hi,
                    jnp.where(sub == first + 1, mid,
                              jnp.where(sub == first + 2, lo,
                                        jnp.where(ones, 1.0, 0.0))))
    pad = jnp.zeros((LANES - hd - SUBLANES, cols), F32)
    data = xt[h * hd:(h + 1) * hd, :]
    parts = [aug, pad, data] if upper else [data, aug, pad]
    return jnp.concatenate(parts, axis=0).astype(BF16)


def _kv_prompt_kernel(x_ref, g_ref, wkvt_ref, bfc_ref,
                      kt_ref, vt_ref, lft_ref, ct_ref, kp_ref, vtb_ref, carry_c, *, n_heads):
    x = x_ref[0]
    tt, d = x.shape
    hd = d // n_heads
    xn = _rms(x, g_ref[...]).astype(BF16)
    nt_dot = lambda lo, hi: lax.dot_general(wkvt_ref[lo:hi, :], xn, _NT,
                                            preferred_element_type=F32)
    kt = nt_dot(0, d)
    vt = nt_dot(d, 2 * d)
    lft = _log_sigmoid(nt_dot(2 * d, 2 * d + n_heads) + bfc_ref[...])
    kt_ref[0] = kt
    vt_ref[0] = vt
    vtb_ref[0] = vt.astype(BF16)
    lft_ref[0] = lft

    @pl.when(pl.program_id(1) == 0)
    def _():
        carry_c[...] = jnp.zeros_like(carry_c)

    r = lax.broadcasted_iota(jnp.int32, (tt, tt), 0)
    c = lax.broadcasted_iota(jnp.int32, (tt, tt), 1)
    cst = _exact_dot_rhs(lft, jnp.where(r <= c, 1.0, 0.0).astype(BF16)) + carry_c[...]
    carry_c[...] = cst[:, tt - 1:tt]
    ct_ref[0] = cst
    for h in range(n_heads):
        kp_ref[0, h] = _head_block_t(kt, cst[h:h + 1, :], h, hd, key_side=True).T


def _kv_prompt(x, g, wkvt, bfc, *, tt, n_heads):
    b, t, d = x.shape
    nt = t // tt
    assert nt * tt == t
    feat = lambda n, dt: (pl.BlockSpec((1, n, tt), lambda i, j: (i, 0, j)),
                          jax.ShapeDtypeStruct((b, n, t), dt))
    outs = [feat(d, F32), feat(d, F32), feat(n_heads, F32), feat(n_heads, F32),
            (pl.BlockSpec((1, n_heads, tt, LANES), lambda i, j: (i, 0, j, 0)),
             jax.ShapeDtypeStruct((b, n_heads, t, LANES), BF16)),
            feat(d, BF16)]
    return pl.pallas_call(
        functools.partial(_kv_prompt_kernel, n_heads=n_heads),
        grid=(b, nt),
        in_specs=[pl.BlockSpec((1, tt, d), lambda i, j: (i, j, 0)), _const_spec((1, d)),
                  _const_spec(wkvt.shape), _const_spec(bfc.shape)],
        out_specs=[o[0] for o in outs],
        out_shape=[o[1] for o in outs],
        scratch_shapes=[pltpu.VMEM((n_heads, 1), F32)],
        compiler_params=_params("parallel", "arbitrary"),
        name="kv_prompt",
    )(x, g.reshape(1, d), wkvt, bfc)


def _qg_body(x, g_ref, wt_ref, ct_ref, qt_ref, gt_ref, n_heads):
    d = x.shape[-1]
    hd = d // n_heads
    xn = _rms(x, g_ref[...]).astype(BF16)
    zt = lax.dot_general(wt_ref[...], xn, _NT, preferred_element_type=F32)
    gt_ref[0] = zt[d:, :].astype(BF16)
    qt = zt[:d, :] * (hd ** -0.5 * LOG2E)
    ct = ct_ref[0]
    for h in range(n_heads):
        qt_ref[0, h] = _head_block_t(qt, ct[h:h + 1, :], h, hd, key_side=False)


def _qg_prompt_kernel(x_ref, g_ref, wt_ref, ct_ref, qt_ref, gt_ref, *, n_heads):
    _qg_body(x_ref[0], g_ref, wt_ref, ct_ref, qt_ref, gt_ref, n_heads)


def _qg_prompt(x, g, wt, ct, *, tt, n_heads):
    b, t, d = x.shape
    nt = t // tt
    assert nt * tt == t
    return pl.pallas_call(
        functools.partial(_qg_prompt_kernel, n_heads=n_heads),
        grid=(b, nt),
        in_specs=[pl.BlockSpec((1, tt, d), lambda i, j: (i, j, 0)), _const_spec((1, d)),
                  _const_spec(wt.shape),
                  pl.BlockSpec((1, n_heads, tt), lambda i, j: (i, 0, j))],
        out_specs=[pl.BlockSpec((1, n_heads, LANES, tt), lambda i, j: (i, 0, 0, j)),
                   pl.BlockSpec((1, d, tt), lambda i, j: (i, 0, j))],
        out_shape=[jax.ShapeDtypeStruct((b, n_heads, LANES, t), BF16),
                   jax.ShapeDtypeStruct((b, d, t), BF16)],
        compiler_params=_params("parallel", "parallel"),
        name="qg_prompt",
    )(x, g.reshape(1, d), wt, ct)


ATTN_FULL2, ATTN_FULL_DIAG, ATTN_DIAG = 0, 1, 2


def _attn_kernel(qi_tab, kj_tab, kind_tab, q_ref, k_ref, v_ref, o_ref, m_s, l_s, al_s, mc_s, acc_s,
                 *bufs, hd):
    st_bufs, pt_bufs = bufs[:4], bufs[4:]
    step = pl.program_id(1)
    kj = kj_tab[step]
    kind = kind_tab[step]
    n_heads, _, tq = q_ref.shape[1:]
    tk = tq
    n_pairs = n_heads // 2

    @pl.when(kj == 0)
    def _():
        m_s[...] = jnp.full(m_s.shape, -jnp.inf, F32)
        l_s[...] = jnp.zeros_like(l_s)
        acc_s[...] = jnp.zeros_like(acc_s)

    def run(diag_flags):
        items = [(kb, pair) for kb in range(len(diag_flags)) for pair in range(n_pairs)]

        def scores(n, s):
            kb, pair = items[n]
            for j in range(2):
                h = 2 * pair + j
                st = jnp.dot(k_ref[0, h, kb * tk:(kb + 1) * tk, :], q_ref[0, h],
                             preferred_element_type=F32)
                if diag_flags[kb]:
                    keep = (lax.broadcasted_iota(jnp.int32, (tk, tq), 0)
                            <= lax.broadcasted_iota(jnp.int32, (tk, tq), 1))
                    st = jnp.where(keep, st, -jnp.inf)
                st_bufs[2 * s + j][...] = st
                mc_s[2 * s + j] = jnp.max(st, axis=0, keepdims=True)

        def softmax(n, s):
            for j in range(2):
                h = 2 * items[n][1] + j
                m_prev = m_s[h]
                m_new = jnp.maximum(m_prev, mc_s[2 * s + j])
                m_s[h] = m_new
                al_s[h] = jnp.exp2(m_prev - m_new)
                pt_bufs[2 * s + j][...] = jnp.exp2(st_bufs[2 * s + j][...] - m_new).astype(BF16)

        ones_rows = jnp.ones((BF16_ROWS, tk), BF16)

        def values(n, s):
            kb, pair = items[n]
            for j in range(2):
                h = 2 * pair + j
                rows = slice(h * hd, (h + 1) * hd)
                pv = jnp.dot(jnp.concatenate([v_ref[0, rows, kb * tk:(kb + 1) * tk], ones_rows],
                                             axis=0),
                             pt_bufs[2 * s + j][...], preferred_element_type=F32)
                alpha = al_s[h]
                acc_s[rows, :] = acc_s[rows, :] * alpha + pv[:hd, :]
                l_s[h] = alpha * l_s[h] + pv[hd:hd + 1, :]

        for t in range(len(items) + 2):
            if t < len(items):
                scores(t, t % 2)
            if 1 <= t <= len(items):
                softmax(t - 1, (t - 1) % 2)
            if t >= 2:
                values(t - 2, t % 2)

    def finish():
        for h in range(n_heads):
            rows = slice(h * hd, (h + 1) * hd)
            o_ref[0, rows, :] = (acc_s[rows, :] * (1.0 / l_s[h])).astype(o_ref.dtype)

    @pl.when(kind == ATTN_FULL2)
    def _():
        run((False, False))

    @pl.when(kind == ATTN_FULL_DIAG)
    def _():
        run((False, True))
        finish()

    @pl.when(kind == ATTN_DIAG)
    def _():
        run((True,))
        finish()


def _prompt_attention(qt, kp, vtb, *, blk, hd):
    b, n_heads, _, t = qt.shape
    d = vtb.shape[1]
    nq = t // blk
    assert nq * blk == t and nq % 2 == 0
    qi_l, kj_l, kind_l = [], [], []
    for i in range(nq):
        for j in range(i // 2 + 1):
            qi_l.append(i)
            kj_l.append(j)
            kind_l.append(ATTN_FULL2 if 2 * j + 1 < i else
                          ATTN_FULL_DIAG if 2 * j + 1 == i else ATTN_DIAG)
    grid_spec = pltpu.PrefetchScalarGridSpec(
        num_scalar_prefetch=3,
        grid=(b, len(qi_l)),
        in_specs=[
            pl.BlockSpec((1, n_heads, LANES, blk), lambda i, s, qt_, kt_, kd_: (i, 0, 0, qt_[s])),
            pl.BlockSpec((1, n_heads, 2 * blk, LANES),
                         lambda i, s, qt_, kt_, kd_: (i, 0, kt_[s], 0)),
            pl.BlockSpec((1, d, 2 * blk), lambda i, s, qt_, kt_, kd_: (i, 0, kt_[s])),
        ],
        out_specs=pl.BlockSpec((1, d, blk), lambda i, s, qt_, kt_, kd_: (i, 0, qt_[s])),
        scratch_shapes=[
            pltpu.VMEM((n_heads, 1, blk), F32),
            pltpu.VMEM((n_heads, 1, blk), F32),
            pltpu.VMEM((n_heads, 1, blk), F32),
            pltpu.VMEM((4, 1, blk), F32),
            pltpu.VMEM((d, blk), F32),
        ] + [pltpu.VMEM((blk, blk), F32)] * 4 + [pltpu.VMEM((blk, blk), BF16)] * 4,
    )
    return pl.pallas_call(
        functools.partial(_attn_kernel, hd=hd),
        grid_spec=grid_spec,
        out_shape=jax.ShapeDtypeStruct((b, d, t), BF16),
        compiler_params=_params("parallel", "arbitrary"),
        name="prompt_attention",
    )(jnp.asarray(qi_l, jnp.int32), jnp.asarray(kj_l, jnp.int32), jnp.asarray(kind_l, jnp.int32),
      qt, kp, vtb)


def _out_body(ot_ref, gt_ref, x_ref, w_ref, post_ref):
    hgt = ot_ref[0].astype(F32) * _silu(gt_ref[0].astype(F32))
    y = lax.dot_general(hgt.astype(BF16), w_ref[...], _TN, preferred_element_type=F32)
    return x_ref[0] + _rms(y, post_ref[...])


def _out_prompt_kernel(ot_ref, gt_ref, x_ref, w_ref, post_ref, y_ref):
    y_ref[0] = _out_body(ot_ref, gt_ref, x_ref, w_ref, post_ref)


def _out_qg_prompt_kernel(ot_ref, gt_ref, x_ref, w_ref, post_ref, g_ref, wt_ref, ct_ref,
                          y_ref, qt_ref, gt_next_ref, *, n_heads):
    y = _out_body(ot_ref, gt_ref, x_ref, w_ref, post_ref)
    y_ref[0] = y
    _qg_body(y, g_ref, wt_ref, ct_ref, qt_ref, gt_next_ref, n_heads)


def _out_qg_prompt(ot, gt, x, w, post, g, wt, ct, *, tt, n_heads):
    b, t, d = x.shape
    nt = t // tt
    assert nt * tt == t
    row_spec = pl.BlockSpec((1, tt, d), lambda i, j: (i, j, 0))
    feat_spec = pl.BlockSpec((1, d, tt), lambda i, j: (i, 0, j))
    return pl.pallas_call(
        functools.partial(_out_qg_prompt_kernel, n_heads=n_heads),
        grid=(b, nt),
        in_specs=[feat_spec, feat_spec, row_spec, _const_spec(w.shape), _const_spec((1, d)),
                  _const_spec((1, d)), _const_spec(wt.shape),
                  pl.BlockSpec((1, n_heads, tt), lambda i, j: (i, 0, j))],
        out_specs=[row_spec,
                   pl.BlockSpec((1, n_heads, LANES, tt), lambda i, j: (i, 0, 0, j)),
                   feat_spec],
        out_shape=[jax.ShapeDtypeStruct((b, t, d), F32),
                   jax.ShapeDtypeStruct((b, n_heads, LANES, t), BF16),
                   jax.ShapeDtypeStruct((b, d, t), BF16)],
        compiler_params=_params("parallel", "parallel"),
        name="out_qg_prompt",
    )(ot, gt, x, w, post.reshape(1, d), g.reshape(1, d), wt, ct)


def _out_prompt(ot, gt, x, wt, post, *, tt):
    b, t, d = x.shape
    nt = t // tt
    assert nt * tt == t
    row_spec = pl.BlockSpec((1, tt, d), lambda i, j: (i, j, 0))
    feat_spec = pl.BlockSpec((1, d, tt), lambda i, j: (i, 0, j))
    return pl.pallas_call(
        _out_prompt_kernel,
        grid=(b, nt),
        in_specs=[feat_spec, feat_spec, row_spec, _const_spec(wt.shape), _const_spec((1, d))],
        out_specs=row_spec,
        out_shape=jax.ShapeDtypeStruct((b, t, d), F32),
        compiler_params=_params("parallel", "parallel"),
        name="out_prompt",
    )(ot, gt, x, wt, post.reshape(1, d))


def _kv_sample_kernel(x_ref, g_ref, wk_ref, wv_ref, wf_ref, bfr_ref, k_ref, v_ref, lf_ref,
                      *, n_heads):
    xn = _rms(x_ref[...], g_ref[...]).astype(BF16)
    k_ref[...] = jnp.dot(xn, wk_ref[...], preferred_element_type=F32)
    v_ref[...] = jnp.dot(xn, wv_ref[...], preferred_element_type=F32)
    lf = _log_sigmoid(jnp.dot(xn, wf_ref[...], preferred_element_type=F32) + bfr_ref[...])
    lf_ref[...] = lf[:, :n_heads]


def _kv_sample(x, g, wk, wv, wf, bfr, *, n_heads):
    rows, d = x.shape
    return pl.pallas_call(
        functools.partial(_kv_sample_kernel, n_heads=n_heads),
        out_shape=[jax.ShapeDtypeStruct((rows, d), F32), jax.ShapeDtypeStruct((rows, d), F32),
                   jax.ShapeDtypeStruct((rows, n_heads), F32)],
        compiler_params=pltpu.CompilerParams(vmem_limit_bytes=VMEM_LIMIT),
        name="kv_sample",
    )(x, g.reshape(1, d), wk, wv, wf, bfr)


def _qg_sample_kernel(x_ref, g_ref, wt_ref, q_ref, gate_ref, *, n_heads):
    d = x_ref.shape[-1]
    xn = _rms(x_ref[...], g_ref[...]).astype(BF16)
    z = lax.dot_general(xn, wt_ref[...], _NT, preferred_element_type=F32)
    q_ref[...] = z[:, :d] * ((d // n_heads) ** -0.5)
    gate_ref[...] = z[:, d:]


def _qg_sample(x, g, w, *, n_heads):
    rows, d = x.shape
    return pl.pallas_call(
        functools.partial(_qg_sample_kernel, n_heads=n_heads),
        out_shape=[jax.ShapeDtypeStruct((rows, d), F32), jax.ShapeDtypeStruct((rows, d), F32)],
        compiler_params=pltpu.CompilerParams(vmem_limit_bytes=VMEM_LIMIT),
        name="qg_sample",
    )(x, g.reshape(1, d), w)


def _out_sample_kernel(o_ref, gate_ref, x_ref, w_ref, post_ref, y_ref):
    hg = o_ref[...] * _silu(gate_ref[...])
    y = jnp.dot(hg.astype(BF16), w_ref[...], preferred_element_type=F32)
    y_ref[...] = x_ref[...] + _rms(y, post_ref[...])


def _out_sample(o, gate, x, w, post):
    rows, d = x.shape
    return pl.pallas_call(
        _out_sample_kernel,
        out_shape=jax.ShapeDtypeStruct((rows, d), F32),
        compiler_params=pltpu.CompilerParams(vmem_limit_bytes=VMEM_LIMIT),
        name="out_sample",
    )(o, gate, x, w, post.reshape(1, d))


PAGE_GROUP = 4
RING_GROUPS = 8
N_SLOTS = RING_GROUPS * PAGE_GROUP


def _decode_kernel(pt_ref, q_ref, kn_ref, vn_ref, lfn_ref, kc_hbm, vc_hbm, lfc_hbm, o_ref,
                   kvbuf, lbuf, sem_kv, sem_l, s_ref, p_ref, acc_ref, qbd_ref, *, n_heads):
    seq = pl.program_id(0)
    n_seq = pl.num_programs(0)
    n_pages = pt_ref.shape[1]
    d, page = kvbuf.shape[1:]
    nq = q_ref.shape[1]
    hd = d // n_heads
    rows = n_heads * nq
    assert rows == LANES and page == LANES and n_pages % PAGE_GROUP == 0
    k_groups = n_pages // PAGE_GROUP
    n_groups = 2 * k_groups

    def kv_copy(src_hbm, page_id, slot):
        return pltpu.make_async_copy(src_hbm.at[page_id], kvbuf.at[slot], sem_kv.at[slot])

    def lf_copy(page_id, slot):
        return pltpu.make_async_copy(lfc_hbm.at[page_id], lbuf.at[slot], sem_l.at[slot])

    first = seq * n_groups

    def group_base(gid):
        return lax.rem(gid, RING_GROUPS) * PAGE_GROUP

    def start_group(gid):
        base = group_base(gid)
        sq = lax.div(gid, n_groups)
        grp = lax.rem(gid, n_groups)
        live = gid < n_seq * n_groups

        @pl.when(live & (grp < k_groups))
        def _():
            for j in range(PAGE_GROUP):
                pid = pt_ref[sq, grp * PAGE_GROUP + j]
                kv_copy(kc_hbm, pid, base + j).start()
                lf_copy(pid, base + j).start()

        @pl.when(live & (grp >= k_groups))
        def _():
            for j in range(PAGE_GROUP):
                pid = pt_ref[sq, (grp - k_groups) * PAGE_GROUP + j]
                kv_copy(vc_hbm, pid, base + j).start()

    @pl.when(seq == 0)
    def _():
        for g in range(RING_GROUPS):
            start_group(jnp.int32(g))

    q = q_ref[0]
    lane_head = lax.broadcasted_iota(jnp.int32, (nq, d), 1) // hd
    qbd_ref[...] = jnp.concatenate([jnp.where(lane_head == h, q, 0.0) for h in range(n_heads)],
                                   axis=0).astype(BF16)

    r = lax.broadcasted_iota(jnp.int32, (page, 2 * page), 0)
    c = lax.broadcasted_iota(jnp.int32, (page, 2 * page), 1)
    tri_ones = jnp.where((r <= c) | (c >= page), 1.0, 0.0).astype(BF16)

    def head_rows(ct):
        return jnp.concatenate(
            [jnp.broadcast_to(ct[h:h + 1, :], (nq, ct.shape[1])) for h in range(n_heads)], axis=0)

    def k_group(g, carry):
        gid = first + g
        base = group_base(gid)
        for j in range(PAGE_GROUP):
            kv_copy(kc_hbm, 0, base + j).wait()
            lf_copy(0, base + j).wait()
        for j in range(PAGE_GROUP):
            sc = jnp.dot(qbd_ref[...], kvbuf[base + j].astype(BF16), preferred_element_type=F32)
            cum = _exact_dot_rhs(lbuf[base + j], tri_ones)
            s_ref[g * PAGE_GROUP + j] = sc - head_rows(carry + cum[:, :page])
            carry = carry + cum[:, page:]
        start_group(gid + RING_GROUPS)
        return carry

    past_total = lax.fori_loop(0, k_groups, k_group, jnp.zeros((n_heads, page), F32))

    ct_new = past_total + _exact_dot_rhs(lfn_ref[0], tri_ones)[:, :page]
    kn = jnp.concatenate([kn_ref[0], jnp.zeros((page - nq, d), F32)], axis=0).astype(BF16)
    sc_new = lax.dot_general(qbd_ref[...], kn, _NT, preferred_element_type=F32)
    key_idx = lax.broadcasted_iota(jnp.int32, (rows, page), 1)
    q_idx = lax.rem(lax.broadcasted_iota(jnp.int32, (rows, page), 0), nq)
    ct_rows = head_rows(ct_new)
    cq = jnp.sum(jnp.where(key_idx == q_idx, ct_rows, 0.0), axis=1, keepdims=True)
    s_ref[n_pages] = jnp.where(key_idx <= q_idx, sc_new - ct_rows, -jnp.inf)

    def max_body(i, m):
        return jnp.maximum(m, s_ref[i])
    m_el = lax.fori_loop(0, n_pages + 1, max_body, jnp.full((rows, page), -jnp.inf, F32))
    m = jnp.max(m_el, axis=1, keepdims=True) + cq

    def p_body(i, l):
        pe = jnp.exp((s_ref[i] + cq) - m)
        p_ref[i] = pe.astype(BF16)
        return l + pe
    l_el = lax.fori_loop(0, n_pages + 1, p_body, jnp.zeros((rows, page), F32))
    l = jnp.sum(l_el, axis=1, keepdims=True)

    vn = jnp.concatenate([vn_ref[0], jnp.zeros((page - nq, d), F32)], axis=0).astype(BF16)
    acc_ref[...] = jnp.dot(p_ref[n_pages], vn, preferred_element_type=F32)

    def v_group(g, carry):
        gid = first + k_groups + g
        base = group_base(gid)
        for j in range(PAGE_GROUP):
            kv_copy(vc_hbm, 0, base + j).wait()
        upd = None
        for j in range(PAGE_GROUP):
            pv = lax.dot_general(p_ref[g * PAGE_GROUP + j], kvbuf[base + j].astype(BF16), _NT,
                                 preferred_element_type=F32)
            upd = pv if upd is None else upd + pv
        acc_ref[...] += upd
        start_group(gid + RING_GROUPS)
        return carry

    lax.fori_loop(0, k_groups, v_group, 0)

    acc = acc_ref[...] * (1.0 / l)
    out = jnp.zeros((nq, d), F32)
    for h in range(n_heads):
        out = out + jnp.where(lane_head == h, acc[h * nq:(h + 1) * nq, :], 0.0)
    o_ref[0] = out


def _decode_attention(page_table, q, k_new, v_new, lft_new, cache_kt, cache_vt, cache_lft, *, n_heads):
    b, nq, d = q.shape
    n_pages = page_table.shape[1]
    page = cache_kt.shape[2]
    rows = n_heads * nq
    seq_spec = pl.BlockSpec((1, nq, d), lambda i, pt: (i, 0, 0))
    grid_spec = pltpu.PrefetchScalarGridSpec(
        num_scalar_prefetch=1,
        grid=(b,),
        in_specs=[
            seq_spec, seq_spec, seq_spec,
            pl.BlockSpec((1, n_heads, page), lambda i, pt: (i, 0, 0)),
            pl.BlockSpec(memory_space=pl.ANY),
            pl.BlockSpec(memory_space=pl.ANY),
            pl.BlockSpec(memory_space=pl.ANY),
        ],
        out_specs=seq_spec,
        scratch_shapes=[
            pltpu.VMEM((N_SLOTS, d, page), F32),
            pltpu.VMEM((N_SLOTS, n_heads, page), F32),
            pltpu.SemaphoreType.DMA((N_SLOTS,)),
            pltpu.SemaphoreType.DMA((N_SLOTS,)),
            pltpu.VMEM((n_pages + 1, rows, page), F32),
            pltpu.VMEM((n_pages + 1, rows, page), BF16),
            pltpu.VMEM((rows, d), F32),
            pltpu.VMEM((rows, d), BF16),
        ],
    )
    return pl.pallas_call(
        functools.partial(_decode_kernel, n_heads=n_heads),
        grid_spec=grid_spec,
        out_shape=jax.ShapeDtypeStruct((b, nq, d), F32),
        compiler_params=_params("arbitrary"),
        name="decode_attention",
    )(page_table, q, k_new, v_new, lft_new, cache_kt, cache_vt, cache_lft)


def kernel(x_prompt, x_sample, cache_k, cache_v, cache_logf, state_rnn, state_conv, page_table,
           a_pre_norm, a_post_norm, a_w_in, a_conv_w, a_conv_b, a_w_ga, a_b_ga, a_w_gx, a_b_gx,
           a_lambda, a_w_out, kv_norm, w_kv, b_f, b_pre_norm, b_post_norm, b_w_in, b_w_out):
    n_a = a_w_in.shape[0]
    n_b = b_w_in.shape[0]
    d = x_prompt.shape[-1]
    n_heads = b_f.shape[0]
    hd = d // n_heads
    assert 2 * hd == LANES and n_heads <= LANES

    a_win = a_w_in.astype(BF16)
    a_wout = a_w_out.astype(BF16)
    a_wg = jnp.concatenate([a_w_ga, a_w_gx], axis=-1).astype(BF16)
    w_kv_t = w_kv.T.astype(BF16)
    wk = w_kv[:, :d].astype(BF16)
    wv = w_kv[:, d:2 * d].astype(BF16)
    wf = jnp.pad(w_kv[:, 2 * d:], ((0, 0), (0, LANES - n_heads))).astype(BF16)
    bfr = jnp.pad(b_f, (0, LANES - n_heads)).reshape(1, LANES)
    bfc = b_f.reshape(n_heads, 1)
    b_wout = b_w_out.astype(BF16)
    b_win_t = jnp.swapaxes(b_w_in, 1, 2).astype(BF16)

    def a_layer(l, x, h0, c0, tt, nb):
        return _layer_a(x, h0, c0, a_pre_norm[l], a_post_norm[l], a_win[l], a_conv_w[l], a_conv_b[l],
                        a_wg[l], a_b_ga[l], a_b_gx[l], a_lambda[l], a_wout[l], tt=tt, nb=nb)

    bp, tp, _ = x_prompt.shape
    x = x_prompt
    rnn_p, conv_p = [], []
    for l in range(n_a):
        x, hl, cn = a_layer(l, x, jnp.zeros((bp, 1, d), F32), jnp.zeros((bp, CONV_W - 1, d), F32),
                            tt=_tile(tp, RGLRU_TILE), nb=1)
        rnn_p.append(hl.reshape(bp, d))
        conv_p.append(cn)
    tt = _tile(tp, PROMPT_TILE)
    kt_p, vt_p, lft_p, ct_p, kaug_p, vtb_p = _kv_prompt(x, kv_norm, w_kv_t, bfc, tt=tt,
                                                        n_heads=n_heads)
    qt, gt = _qg_prompt(x, b_pre_norm[0], b_win_t[0], ct_p, tt=tt, n_heads=n_heads)
    for l in range(n_b):
        ot = _prompt_attention(qt, kaug_p, vtb_p, blk=tt, hd=hd)
        if l + 1 < n_b:
            x, qt, gt = _out_qg_prompt(ot, gt, x, b_wout[l], b_post_norm[l], b_pre_norm[l + 1],
                                       b_win_t[l + 1], ct_p, tt=tt, n_heads=n_heads)
        else:
            x = _out_prompt(ot, gt, x, b_wout[l], b_post_norm[l], tt=tt)
    y_prompt = x
    k_p = jnp.transpose(kt_p.reshape(bp, n_heads, hd, tp), (0, 3, 1, 2))
    v_p = jnp.transpose(vt_p.reshape(bp, n_heads, hd, tp), (0, 3, 1, 2))
    lf_p = jnp.swapaxes(lft_p, 1, 2)

    bs, ts, _ = x_sample.shape
    rows_s = bs * ts
    n_pool, page = cache_k.shape[:2]
    x = jnp.swapaxes(x_sample, 0, 1).reshape(1, rows_s, d)
    rnn_s, conv_s = [], []
    for l in range(n_a):
        c0 = jnp.swapaxes(state_conv[l], 0, 1).reshape(1, (CONV_W - 1) * bs, d)
        x, hl, cn = a_layer(l, x, state_rnn[l].reshape(1, bs, d), c0, tt=ts, nb=bs)
        rnn_s.append(hl.reshape(bs, d))
        conv_s.append(jnp.swapaxes(cn.reshape(CONV_W - 1, bs, d), 0, 1))
    x = jnp.swapaxes(x.reshape(ts, bs, d), 0, 1).reshape(rows_s, d)
    k_s, v_s, lf_s = _kv_sample(x, kv_norm, wk, wv, wf, bfr, n_heads=n_heads)
    lft_new = jnp.pad(jnp.swapaxes(lf_s.reshape(bs, ts, n_heads), 1, 2),
                      ((0, 0), (0, 0), (0, page - ts)))
    cache_kt = jnp.transpose(cache_k, (0, 2, 3, 1)).reshape(n_pool, d, page)
    cache_vt = jnp.transpose(cache_v, (0, 2, 3, 1)).reshape(n_pool, d, page)
    cache_lft = jnp.swapaxes(cache_logf, 1, 2)
    for l in range(n_b):
        q, gate = _qg_sample(x, b_pre_norm[l], b_win_t[l], n_heads=n_heads)
        o = _decode_attention(page_table, q.reshape(bs, ts, d), k_s.reshape(bs, ts, d),
                              v_s.reshape(bs, ts, d), lft_new, cache_kt, cache_vt, cache_lft,
                              n_heads=n_heads)
        x = _out_sample(o.reshape(rows_s, d), gate, x, b_wout[l], b_post_norm[l])
    y_sample = x.reshape(bs, ts, d)

    return (y_prompt, y_sample, k_p, v_p, lf_p,
            jnp.stack(rnn_p), jnp.stack(conv_p),
            k_s.reshape(bs, ts, n_heads, hd), v_s.reshape(bs, ts, n_heads, hd),
            lf_s.reshape(bs, ts, n_heads),
            jnp.stack(rnn_s), jnp.stack(conv_s))
```
